```python
import math
import jax
import jax.numpy as jnp
from jax import lax
import numpy as np

D_MODEL = 1024
BATCH = 8
SEQ = 2048
DEPTH = 4
DEC_BATCH = 128
DEC_SEQ = 4
PAST_LEN = 2048
PAGE_SIZE = 128

N_EVEN = (DEPTH + 1) // 2
N_ODD = DEPTH // 2
A_HEADS = 4
A_HD = D_MODEL // (4 * A_HEADS)
A_VD = 2 * A_HD
A_QK_W = A_HEADS * 2 * A_HD
A_V_W = A_HEADS * A_VD
POOL_WINDOWS = (2, 4, 8, 16)
POOL_GROUPS = len(POOL_WINDOWS)
POOL_W = D_MODEL - A_V_W
POOL_GC = POOL_W // POOL_GROUPS
POOL_BUF = max(POOL_WINDOWS) - 1
C_HD = 64
C_HEADS = D_MODEL // C_HD
C_W = C_HEADS * C_HD
ROT_DIM = A_HD // 4
ROPE_THETA = 500000.0
D_FF = ((8 * D_MODEL // 3 + 127) // 128) * 128
CONV_W = 3
Q_BLOCK = 128
EPS = 1e-6
NEG_INF = -1e30
GATE_BIAS = 3.0

kernel_name = "hybrid_diffattn_pool_fox_convffn_step"


def rmsnorm(x, g):
    xf = x.astype(jnp.float32)
    y = xf * lax.rsqrt(jnp.mean(xf * xf, axis=-1, keepdims=True) + EPS)
    return (y * g.astype(jnp.float32)).astype(x.dtype)


def apply_partial_rope(x, pos):
    half = ROT_DIM // 2
    inv = ROPE_THETA ** (-jnp.arange(0, ROT_DIM, 2, dtype=jnp.float32) / ROT_DIM)
    ang = pos.astype(jnp.float32)[:, None] * inv[None, :]
    ang = ang.reshape((ang.shape[0],) + (1,) * (x.ndim - 3) + (half,))
    cos, sin = jnp.cos(ang), jnp.sin(ang)
    xf = x[..., :ROT_DIM].astype(jnp.float32)
    x1, x2 = xf[..., :half], xf[..., half:]
    rot = jnp.concatenate([x1 * cos - x2 * sin, x2 * cos + x1 * sin], axis=-1)
    return jnp.concatenate([rot.astype(x.dtype), x[..., ROT_DIM:]], axis=-1)


def gather_pages(cache_l, page_table):
    g = cache_l[page_table]
    return g.reshape((g.shape[0], g.shape[1] * g.shape[2]) + cache_l.shape[2:])


def sweep_query_blocks(fn, q_pos, *q_args):
    T = q_pos.shape[0]
    if T % Q_BLOCK != 0:
        return fn(q_pos, *q_args)
    nb = T // Q_BLOCK

    def split(a):
        return jnp.moveaxis(a.reshape((a.shape[0], nb, Q_BLOCK) + a.shape[2:]), 1, 0)

    out = lax.map(lambda args: fn(*args), (q_pos.reshape(nb, Q_BLOCK),) + tuple(split(a) for a in q_args))
    out = jnp.moveaxis(out, 0, 1)
    return out.reshape((out.shape[0], T) + out.shape[3:])


def pool_mix(p, prefix, pos, pool_w, pool_scale):
    B, T, C = p.shape
    full = jnp.concatenate([prefix.astype(p.dtype), p], axis=1)
    cs = jnp.cumsum(full.astype(jnp.float32), axis=1)
    cs = jnp.pad(cs, ((0, 0), (1, 0), (0, 0)))
    outs = []
    for g, w in enumerate(POOL_WINDOWS):
        sl = slice(g * POOL_GC, (g + 1) * POOL_GC)
        upper = cs[:, POOL_BUF + 1:POOL_BUF + 1 + T, sl]
        lower = cs[:, POOL_BUF + 1 - w:POOL_BUF + 1 - w + T, sl]
        cnt = jnp.minimum(w, pos + 1).astype(jnp.float32)[None, :, None]
        outs.append((upper - lower) / cnt - p[..., sl].astype(jnp.float32))
    d = jnp.concatenate(outs, axis=-1).reshape(B, T, POOL_GROUPS, POOL_GC)
    y = jnp.einsum('btgc,gce->btge', d, pool_w.astype(jnp.float32)).reshape(B, T, C)
    y = y * pool_scale.astype(jnp.float32)
    return y.astype(p.dtype), full[:, -POOL_BUF:]


def diff_pool_mixer(h, pos, k_past, v_past, pool_prefix, w_in, diff_lambda, subln_g,
                    pool_w, pool_scale, w_out, lam_init):
    B, T, _ = h.shape
    z = h @ w_in
    q, k, v, p = jnp.split(z, [A_QK_W, 2 * A_QK_W, 2 * A_QK_W + A_V_W], axis=-1)
    q = apply_partial_rope(q.reshape(B, T, A_HEADS, 2, A_HD), pos)
    k = apply_partial_rope(k.reshape(B, T, A_HEADS, 2, A_HD), pos)
    v = v.reshape(B, T, A_HEADS, A_VD)
    if k_past is None:
        k_all, v_all = k, v
    else:
        k_all = jnp.concatenate([k_past.reshape(B, -1, A_HEADS, 2, A_HD).astype(k.dtype), k], axis=1)
        v_all = jnp.concatenate([v_past.astype(v.dtype), v], axis=1)
    k_pos = jnp.arange(k_all.shape[1])
    dl = diff_lambda.astype(jnp.float32)
    lam = jnp.exp(jnp.sum(dl[0] * dl[1])) - jnp.exp(jnp.sum(dl[2] * dl[3])) + lam_init

    def core(q_pos, qb):
        s = jnp.einsum('bqhcd,bkhcd->bhcqk', qb, k_all).astype(jnp.float32) * (A_HD ** -0.5)
        s = jnp.where(k_pos[None, :] <= q_pos[:, None], s, NEG_INF)
        pm = jax.nn.softmax(s, axis=-1)
        pd = pm[:, :, 0] - lam * pm[:, :, 1]
        return jnp.einsum('bhqk,bkhe->bqhe', pd.astype(v_all.dtype), v_all)

    o = sweep_query_blocks(core, pos, q)
    o = (rmsnorm(o, subln_g) * (1.0 - lam_init)).astype(h.dtype).reshape(B, T, A_V_W)
    pooled, pool_new = pool_mix(p, pool_prefix, pos, pool_w, pool_scale)
    y = jnp.concatenate([o, pooled], axis=-1) @ w_out
    return y, k.reshape(B, T, A_HEADS, 2 * A_HD), v, pool_new


def fox_mixer(h, pos, k_past, v_past, logf_past, w_in, b_f, w_out):
    B, T, _ = h.shape
    z = h @ w_in
    q, k, v, fl = jnp.split(z, [C_W, 2 * C_W, 3 * C_W], axis=-1)
    q = q.reshape(B, T, C_HEADS, C_HD)
    k = k.reshape(B, T, C_HEADS, C_HD)
    v = v.reshape(B, T, C_HEADS, C_HD)
    logf = jax.nn.log_sigmoid((fl + b_f).astype(jnp.float32))
    if k_past is None:
        k_all, v_all, logf_all = k, v, logf
    else:
        k_all = jnp.concatenate([k_past.astype(k.dtype), k], axis=1)
        v_all = jnp.concatenate([v_past.astype(v.dtype), v], axis=1)
        logf_all = jnp.concatenate([logf_past.astype(jnp.float32), logf], axis=1)
    cum = jnp.cumsum(logf_all, axis=1)
    ck = jnp.transpose(cum, (0, 2, 1))
    cq = cum[:, -T:]
    k_pos = jnp.arange(k_all.shape[1])

    def core(q_pos, qb, cqb):
        s = jnp.einsum('bqhd,bkhd->bhqk', qb, k_all).astype(jnp.float32) * (C_HD ** -0.5)
        s = s + jnp.transpose(cqb, (0, 2, 1))[..., None] - ck[:, :, None, :]
        s = jnp.where(k_pos[None, :] <= q_pos[:, None], s, NEG_INF)
        pm = jax.nn.softmax(s, axis=-1)
        return jnp.einsum('bhqk,bkhd->bqhd', pm.astype(v_all.dtype), v_all)

    o = sweep_query_blocks(core, pos, q, cq)
    y = o.reshape(B, T, C_W) @ w_out
    return y, k, v, logf


def conv_ffn(h, prefix, w_up, conv_w, conv_b, w_down):
    T = h.shape[1]
    u = h @ w_up
    full = jnp.concatenate([prefix.astype(u.dtype), u], axis=1)
    c = conv_b + sum(conv_w[j] * full[:, j:j + T] for j in range(CONV_W))
    g, val = jnp.split(c, 2, axis=-1)
    y = (jax.nn.silu(g) * val) @ w_down
    return y, full[:, -(CONV_W - 1):]


def setup_inputs(seed: int = 0) -> dict:
    key = jax.random.key(seed)
    ks = jax.random.split(key, 32)
    f32 = jnp.float32

    def nrm(k, shape, scale):
        return jax.random.normal(k, shape, f32) * scale

    n_pages = PAST_LEN // PAGE_SIZE
    n_used = DEC_BATCH * n_pages
    n_phys = n_used + n_used // 4
    perm = jax.random.permutation(ks[0], n_phys)
    page_table = perm[:n_used].reshape(DEC_BATCH, n_pages).astype(jnp.int32)
    in_ab = 2 * A_QK_W + A_V_W + POOL_W
    in_c = 3 * C_W + C_HEADS
    return {
        "x_prompt": nrm(ks[1], (BATCH, SEQ, D_MODEL), 1.0),
        "x_sample": nrm(ks[2], (DEC_BATCH, DEC_SEQ, D_MODEL), 1.0),
        "cache_diff_k": nrm(ks[3], (N_EVEN, n_phys, PAGE_SIZE, A_HEADS, 2 * A_HD), 1.0),
        "cache_diff_v": nrm(ks[4], (N_EVEN, n_phys, PAGE_SIZE, A_HEADS, A_VD), 1.0),
        "state_pool": nrm(ks[5], (N_EVEN, DEC_BATCH, POOL_BUF, POOL_W), 1.0),
        "cache_fox_k": nrm(ks[6], (N_ODD, n_phys, PAGE_SIZE, C_HEADS, C_HD), 1.0),
        "cache_fox_v": nrm(ks[7], (N_ODD, n_phys, PAGE_SIZE, C_HEADS, C_HD), 1.0),
        "cache_fox_logf": jax.nn.log_sigmoid(GATE_BIAS + nrm(ks[8], (N_ODD, n_phys, PAGE_SIZE, C_HEADS), 1.0)),
        "state_ffn_conv": nrm(ks[9], (DEPTH, DEC_BATCH, CONV_W - 1, 2 * D_FF), 1.0),
        "page_table": page_table,
        "w_in_ab": nrm(ks[10], (N_EVEN, D_MODEL, in_ab), D_MODEL ** -0.5),
        "diff_lambda": nrm(ks[11], (N_EVEN, 4, A_HD), 0.1),
        "diff_subln_g": 1.0 + nrm(ks[12], (N_EVEN, A_VD), 0.02),
        "pool_w": nrm(ks[13], (N_EVEN, POOL_GROUPS, POOL_GC, POOL_GC), POOL_GC ** -0.5),
        "pool_scale": 1.0 + nrm(ks[14], (N_EVEN, POOL_W), 0.02),
        "w_out_ab": nrm(ks[15], (N_EVEN, A_V_W + POOL_W, D_MODEL), (A_V_W + POOL_W) ** -0.5),
        "w_in_c": nrm(ks[16], (N_ODD, D_MODEL, in_c), D_MODEL ** -0.5),
        "b_f": GATE_BIAS + nrm(ks[17], (N_ODD, C_HEADS), 0.1),
        "w_out_c": nrm(ks[18], (N_ODD, C_W, D_MODEL), C_W ** -0.5),
        "norm_mix_pre": 1.0 + nrm(ks[19], (DEPTH, D_MODEL), 0.02),
        "norm_mix_post": 1.0 + nrm(ks[20], (DEPTH, D_MODEL), 0.02),
        "norm_ffn_pre": 1.0 + nrm(ks[21], (DEPTH, D_MODEL), 0.02),
        "norm_ffn_post": 1.0 + nrm(ks[22], (DEPTH, D_MODEL), 0.02),
        "w_up": nrm(ks[23], (DEPTH, D_MODEL, 2 * D_FF), D_MODEL ** -0.5),
        "conv_w": nrm(ks[24], (DEPTH, CONV_W, 2 * D_FF), CONV_W ** -0.5),
        "conv_b": nrm(ks[25], (DEPTH, 2 * D_FF), 0.02),
        "w_down": nrm(ks[26], (DEPTH, D_FF, D_MODEL), D_FF ** -0.5),
    }


def reference(x_prompt, x_sample, cache_diff_k, cache_diff_v, state_pool, cache_fox_k, cache_fox_v,
              cache_fox_logf, state_ffn_conv, page_table, w_in_ab, diff_lambda, diff_subln_g, pool_w,
              pool_scale, w_out_ab, w_in_c, b_f, w_out_c, norm_mix_pre, norm_mix_post, norm_ffn_pre,
              norm_ffn_post, w_up, conv_w, conv_b, w_down):

    def trunk(x, pos, use_cache):
        B = x.shape[0]
        dk, dv, pl, fk, fv, fl, cv = [], [], [], [], [], [], []
        for l in range(DEPTH):
            i = l // 2
            h = rmsnorm(x, norm_mix_pre[l])
            if l % 2 == 0:
                lam_init = 0.8 - 0.6 * math.exp(-0.3 * l)
                if use_cache:
                    k_past = gather_pages(cache_diff_k[i], page_table)
                    v_past = gather_pages(cache_diff_v[i], page_table)
                    prefix = state_pool[i]
                else:
                    k_past, v_past = None, None
                    prefix = jnp.zeros((B, POOL_BUF, POOL_W), x.dtype)
                y, k_new, v_new, pool_new = diff_pool_mixer(
                    h, pos, k_past, v_past, prefix, w_in_ab[i], diff_lambda[i], diff_subln_g[i],
                    pool_w[i], pool_scale[i], w_out_ab[i], lam_init)
                dk.append(k_new)
                dv.append(v_new)
                pl.append(pool_new)
            else:
                if use_cache:
                    k_past = gather_pages(cache_fox_k[i], page_table)
                    v_past = gather_pages(cache_fox_v[i], page_table)
                    lf_past = gather_pages(cache_fox_logf[i], page_table)
                else:
                    k_past, v_past, lf_past = None, None, None
                y, k_new, v_new, lf_new = fox_mixer(h, pos, k_past, v_past, lf_past,
                                                    w_in_c[i], b_f[i], w_out_c[i])
                fk.append(k_new)
                fv.append(v_new)
                fl.append(lf_new)
            x = x + rmsnorm(y, norm_mix_post[l]).astype(x.dtype)
            h = rmsnorm(x, norm_ffn_pre[l])
            if use_cache:
                cprefix = state_ffn_conv[l]
            else:
                cprefix = jnp.zeros((B, CONV_W - 1, 2 * D_FF), x.dtype)
            y, conv_new = conv_ffn(h, cprefix, w_up[l], conv_w[l], conv_b[l], w_down[l])
            cv.append(conv_new)
            x = x + rmsnorm(y, norm_ffn_post[l]).astype(x.dtype)
        return (x, jnp.stack(dk), jnp.stack(dv), jnp.stack(pl), jnp.stack(fk), jnp.stack(fv),
                jnp.stack(fl), jnp.stack(cv))

    y_p, dk_p, dv_p, pool_p, fk_p, fv_p, fl_p, cv_p = trunk(
        x_prompt, jnp.arange(x_prompt.shape[1]), False)
    y_s, dk_s, dv_s, pool_s, fk_s, fv_s, fl_s, cv_s = trunk(
        x_sample, PAST_LEN + jnp.arange(x_sample.shape[1]), True)
    return (y_p, y_s, dk_p, dv_p, pool_p, fk_p, fv_p, fl_p, cv_p,
            dk_s, dv_s, pool_s, fk_s, fv_s, fl_s, cv_s)
```

```python
import functools
import math

import jax
import jax.numpy as jnp
from jax import lax
from jax.experimental import pallas as pl
from jax.experimental.pallas import tpu as pltpu

D_MODEL = 1024
BATCH = 8
SEQ = 2048
DEPTH = 4
DEC_BATCH = 128
DEC_SEQ = 4
PAST_LEN = 2048
PAGE_SIZE = 128
N_PAGES = PAST_LEN // PAGE_SIZE
A_HEADS = 4
A_HD = 64
A_VD = 128
A_QK_W = 512
A_V_W = 512
POOL_WINDOWS = (2, 4, 8, 16)
POOL_W = 512
POOL_GC = 128
POOL_BUF = 15
C_HD = 64
C_HEADS = 16
C_W = 1024
ROT_DIM = 16
ROPE_THETA = 500000.0
D_FF = 2816
CONV_W = 3
EPS = 1e-6
NEG_INF = -1e30

LANES = 128
VMEM_LIMIT = 56 * 1024 * 1024

F32 = jnp.float32
BF16 = jnp.bfloat16
NT_DIMS = (((1,), (1,)), ((), ()))


def _params(sem, vmem=VMEM_LIMIT):
    return pltpu.CompilerParams(dimension_semantics=sem, vmem_limit_bytes=vmem)


def _rms(x, g):
    return x * lax.rsqrt(jnp.mean(x * x, axis=-1, keepdims=True) + EPS) * g


def _dot(a, b):
    return jnp.dot(a, b, preferred_element_type=F32)


def _dot_nt(a, b):
    return lax.dot_general(a, b, NT_DIMS, preferred_element_type=F32)


def _log_sigmoid(x):
    return jnp.minimum(x, 0.0) - jnp.log1p(jnp.exp(-jnp.abs(x)))


def _shift_down(x, s, row):
    return jnp.where(row >= s, pltpu.roll(x, s, 0), 0.0)


def _cumsum(x, axis):
    n = x.shape[axis]
    idx = lax.broadcasted_iota(jnp.int32, x.shape, axis)
    s = 1
    while s < n:
        x = x + jnp.where(idx >= s, pltpu.roll(x, s, axis), 0.0)
        s *= 2
    return x


def _rope(z, c_ref, s1_ref, s2_ref):
    outs = []
    for c in range(z.shape[1] // LANES):
        zc = z[:, c * LANES:(c + 1) * LANES]
        outs.append(zc * c_ref[...] + pltpu.roll(zc, ROT_DIM // 2, 1) * s1_ref[...]
                    + pltpu.roll(zc, LANES - ROT_DIM // 2, 1) * s2_ref[...])
    return jnp.concatenate(outs, axis=1)


def _inproj_even_kernel(x_ref, g_ref, w_ref, c_ref, s1_ref, s2_ref,
                        q_ref, k_ref, v_ref, p_ref, h_scr):
    j = pl.program_id(1)

    @pl.when(j == 0)
    def _():
        h_scr[...] = _rms(x_ref[...], g_ref[...]).astype(BF16)

    z = _dot(h_scr[...], w_ref[...])

    @pl.when(j == 0)
    def _():
        q_ref[...] = (_rope(z, c_ref, s1_ref, s2_ref) * (A_HD ** -0.5)).astype(BF16)

    @pl.when(j == 1)
    def _():
        k_ref[...] = _rope(z, c_ref, s1_ref, s2_ref)

    @pl.when(j == 2)
    def _():
        v_ref[...] = z

    @pl.when(j == 3)
    def _():
        p_ref[...] = z


def _inproj_even(x, g, w, tabs, tm):
    m = x.shape[0]
    ntab = tabs[0].shape[0] // tm
    tok = lambda i, j: (i, 0)
    tab = pl.BlockSpec((tm, LANES), lambda i, j: (i % ntab, 0))
    out = pl.BlockSpec((tm, A_QK_W), tok)
    return pl.pallas_call(
        _inproj_even_kernel,
        grid=(m // tm, 4),
        in_specs=[pl.BlockSpec((tm, D_MODEL), tok),
                  pl.BlockSpec((1, D_MODEL), lambda i, j: (0, 0)),
                  pl.BlockSpec((D_MODEL, A_QK_W), lambda i, j: (0, j)),
                  tab, tab, tab],
        out_specs=[out, out, out, out],
        out_shape=[jax.ShapeDtypeStruct((m, A_QK_W), BF16)]
        + [jax.ShapeDtypeStruct((m, A_QK_W), F32)] * 3,
        scratch_shapes=[pltpu.VMEM((tm, D_MODEL), BF16)],
        compiler_params=_params(("parallel", "arbitrary")),
        name="inproj_even",
    )(x, g, w, *tabs)


def _inproj_odd_kernel(x_ref, g_ref, w_ref, wf_ref, wft_ref, bf_ref, bft_ref,
                       q_ref, k_ref, v_ref, lf_ref, lft_ref, h_scr):
    j = pl.program_id(1)

    @pl.when(j == 0)
    def _():
        h = _rms(x_ref[...], g_ref[...]).astype(BF16)
        h_scr[...] = h
        lf_ref[...] = _log_sigmoid(_dot(h, wf_ref[...]) + bf_ref[...])
        lft_ref[...] = _log_sigmoid(_dot_nt(wft_ref[...], h) + bft_ref[...])

    z = _dot(h_scr[...], w_ref[...])

    @pl.when(j == 0)
    def _():
        q_ref[...] = (z * (C_HD ** -0.5)).astype(BF16)

    @pl.when(j == 1)
    def _():
        k_ref[...] = z

    @pl.when(j == 2)
    def _():
        v_ref[...] = z


def _inproj_odd(x, g, w_qkv, w_f, w_ft, b_f, tm):
    m = x.shape[0]
    tok = lambda i, j: (i, 0)
    const = lambda i, j: (0, 0)
    out = pl.BlockSpec((tm, C_W), tok)
    return pl.pallas_call(
        _inproj_odd_kernel,
        grid=(m // tm, 3),
        in_specs=[pl.BlockSpec((tm, D_MODEL), tok),
                  pl.BlockSpec((1, D_MODEL), const),
                  pl.BlockSpec((D_MODEL, C_W), lambda i, j: (0, j)),
                  pl.BlockSpec((D_MODEL, C_HEADS), const),
                  pl.BlockSpec((C_HEADS, D_MODEL), const),
                  pl.BlockSpec((1, C_HEADS), const),
                  pl.BlockSpec((C_HEADS, 1), const)],
        out_specs=[out, out, out,
                   pl.BlockSpec((tm, C_HEADS), tok),
                   pl.BlockSpec((C_HEADS, tm), lambda i, j: (0, i))],
        out_shape=[jax.ShapeDtypeStruct((m, C_W), BF16),
                   jax.ShapeDtypeStruct((m, C_W), F32),
                   jax.ShapeDtypeStruct((m, C_W), F32),
                   jax.ShapeDtypeStruct((m, C_HEADS), F32),
                   jax.ShapeDtypeStruct((C_HEADS, m), F32)],
        scratch_shapes=[pltpu.VMEM((tm, D_MODEL), BF16)],
        compiler_params=_params(("parallel", "arbitrary")),
        name="inproj_odd",
    )(x, g, w_qkv, w_f, w_ft, b_f.reshape(1, C_HEADS), b_f.reshape(C_HEADS, 1))


def _diff_lambda(dl_ref, lam_init):
    dl = dl_ref[...]
    a = jnp.sum(dl[0:1] * dl[1:2], axis=1, keepdims=True)
    b = jnp.sum(dl[2:3] * dl[3:4], axis=1, keepdims=True)
    return jnp.exp(a) - jnp.exp(b) + lam_init


def _attn_prompt_kernel(fox, lam_init, tq, tk, *refs):
    if fox:
        q_ref, k_ref, v_ref, cq_ref, ck_ref, o_ref, kb_scr, vb_scr = refs
    else:
        q_ref, k_ref, v_ref, dl_ref, g_ref, o_ref, kb_scr, vb_scr = refs
    iq = pl.program_id(2)

    @pl.when(iq == 0)
    def _():
        kb_scr[...] = k_ref[...].astype(BF16)
        vb_scr[...] = v_ref[...].astype(BF16)

    q = q_ref[...]
    lane = lax.broadcasted_iota(jnp.int32, q.shape, 1)
    zero = jnp.zeros_like(q)
    q2 = jnp.concatenate([jnp.where(lane < A_HD, q, zero),
                          jnp.where(lane >= A_HD, q, zero)], axis=0)
    if fox:
        cq = cq_ref[...]
        cq2 = jnp.concatenate([cq[:, 0:1], cq[:, 1:2]], axis=0)

    def scores(ik):
        start = pl.multiple_of(ik * tk, tk)
        s = _dot_nt(q2, kb_scr[pl.ds(start, tk), :])
        if fox:
            ck = ck_ref[ik]
            ck2 = jnp.concatenate([jnp.broadcast_to(ck[0:1], (tq, tk)),
                                   jnp.broadcast_to(ck[1:2], (tq, tk))], axis=0)
            s = s + cq2 - ck2
        return s, start

    def update(carry, s, start):
        m, l, acc = carry
        m_new = jnp.maximum(m, jnp.max(s, axis=1, keepdims=True))
        alpha = jnp.exp(m - m_new)
        p = jnp.exp(s - m_new)
        l = alpha * l + jnp.sum(p, axis=1, keepdims=True)
        acc = alpha * acc + _dot(p.astype(BF16), vb_scr[pl.ds(start, tk), :])
        return m_new, l, acc

    def body(ik, carry):
        s, start = scores(ik)
        return update(carry, s, start)

    init = (jnp.full((2 * tq, 1), NEG_INF, F32), jnp.zeros((2 * tq, 1), F32),
            jnp.zeros((2 * tq, LANES), F32))
    carry = lax.fori_loop(0, iq, body, init)
    s, start = scores(iq)
    row = lax.broadcasted_iota(jnp.int32, s.shape, 0) % tq
    col = lax.broadcasted_iota(jnp.int32, s.shape, 1)
    s = jnp.where(col <= row, s, NEG_INF)
    m, l, acc = update(carry, s, start)
    on = acc / l
    top, bot = on[:tq], on[tq:]
    if fox:
        o_ref[...] = jnp.where(lane < C_HD, top, bot)
    else:
        o = top - _diff_lambda(dl_ref, lam_init) * bot
        o_ref[...] = _rms(o, g_ref[...]) * (1.0 - lam_init)


def _attn_prompt(fox, lam_init, q, k, v, extra, tq=512):
    b, t, w = k.shape
    ng, nq = w // LANES, t // tq
    tile = lambda bi, g, iq: (bi, iq, g)
    seq = lambda bi, g, iq: (bi, 0, g)
    if fox:
        cq, ck = extra
        extra_specs = [pl.BlockSpec((None, None, tq, 2), lambda bi, g, iq: (bi, g, iq, 0)),
                       pl.BlockSpec((None, None, nq, 2, tq), lambda bi, g, iq: (bi, g, 0, 0, 0))]
    else:
        extra_specs = [pl.BlockSpec((4, A_HD), lambda bi, g, iq: (0, 0)),
                       pl.BlockSpec((1, A_VD), lambda bi, g, iq: (0, 0))]
    return pl.pallas_call(
        functools.partial(_attn_prompt_kernel, fox, lam_init, tq, tq),
        grid=(b, ng, nq),
        in_specs=[pl.BlockSpec((None, tq, LANES), tile),
                  pl.BlockSpec((None, t, LANES), seq),
                  pl.BlockSpec((None, t, LANES), seq)] + extra_specs,
        out_specs=pl.BlockSpec((None, tq, LANES), tile),
        out_shape=jax.ShapeDtypeStruct((b, t, w), F32),
        scratch_shapes=[pltpu.VMEM((t, LANES), BF16), pltpu.VMEM((t, LANES), BF16)],
        compiler_params=_params(("parallel", "parallel", "arbitrary")),
        name="fox_attn_prompt" if fox else "diff_attn_prompt",
    )(q, k, v, *extra)


def _fox_cum_prompt_kernel(lf_ref, lft_ref, cum_ref, cumt_ref):
    cum_ref[...] = _cumsum(lf_ref[...], 0)
    cumt_ref[...] = _cumsum(lft_ref[...], 1)


def _fox_cum_prompt(lf, lft):
    b, t, h = lf.shape
    return pl.pallas_call(
        _fox_cum_prompt_kernel,
        grid=(b,),
        in_specs=[pl.BlockSpec((None, t, h), lambda i: (i, 0, 0)),
                  pl.BlockSpec((h, t), lambda i: (0, i))],
        out_specs=[pl.BlockSpec((None, t, h), lambda i: (i, 0, 0)),
                   pl.BlockSpec((None, h, t), lambda i: (i, 0, 0))],
        out_shape=[jax.ShapeDtypeStruct((b, t, h), F32),
                   jax.ShapeDtypeStruct((b, h, t), F32)],
        compiler_params=_params(("parallel",)),
        name="fox_cum_prompt",
    )(lf, lft)


KEYS_PAD = PAST_LEN + PAGE_SIZE


def _decode_scores(q_ref, kn_ref, k_refs, nblk):
    w = q_ref.shape[1]
    rows = DEC_SEQ * nblk
    q4 = q_ref[...]
    qb = jnp.concatenate([jnp.broadcast_to(q4[t:t + 1], (nblk, w)) for t in range(DEC_SEQ)], axis=0)
    rblk = lax.broadcasted_iota(jnp.int32, (rows, w), 0) % nblk
    lblk = lax.broadcasted_iota(jnp.int32, (rows, w), 1) // A_HD
    q2 = jnp.where(rblk == lblk, qb, 0.0).astype(BF16)
    parts = [_dot_nt(q2, kr[...].astype(BF16)) for kr in k_refs]
    kn = jnp.concatenate([kn_ref[...], jnp.zeros((PAGE_SIZE - DEC_SEQ, w), F32)], axis=0)
    parts.append(_dot_nt(q2, kn.astype(BF16)))
    return jnp.concatenate(parts, axis=1)


def _decode_softmax_pv(s, vn_ref, v_refs, nblk):
    rows, w = s.shape[0], vn_ref.shape[1]
    t_row = lax.broadcasted_iota(jnp.int32, s.shape, 0) // nblk
    col = lax.broadcasted_iota(jnp.int32, s.shape, 1)
    s = jnp.where(col <= PAST_LEN + t_row, s, NEG_INF)
    m = jnp.max(s, axis=1, keepdims=True)
    p = jnp.exp(s - m)
    l = jnp.sum(p, axis=1, keepdims=True)
    pb = p.astype(BF16)
    vn = jnp.concatenate([vn_ref[...], jnp.zeros((PAGE_SIZE - DEC_SEQ, w), F32)], axis=0)
    o = _dot(pb[:, PAST_LEN:], vn.astype(BF16))
    for j, vr in enumerate(v_refs):
        o = o + _dot(pb[:, j * PAGE_SIZE:(j + 1) * PAGE_SIZE], vr[...].astype(BF16))
    return o / l


def _group_rows(x, nblk):
    return jnp.concatenate([jnp.sum(x[t * nblk:(t + 1) * nblk], axis=0, keepdims=True)
                            for t in range(DEC_SEQ)], axis=0)


def _decode_diff_kernel(lam_init, pt_ref, q_ref, kn_ref, vn_ref, dl_ref, g_ref, *refs):
    k_refs, v_refs, o_ref = refs[:N_PAGES], refs[N_PAGES:2 * N_PAGES], refs[2 * N_PAGES]
    nblk = 2 * A_HEADS
    s = _decode_scores(q_ref, kn_ref, k_refs, nblk)
    on = _decode_softmax_pv(s, vn_ref, v_refs, nblk)
    r = lax.broadcasted_iota(jnp.int32, on.shape, 0) % nblk
    lane_head = lax.broadcasted_iota(jnp.int32, on.shape, 1) // A_VD
    lam = _diff_lambda(dl_ref, lam_init)
    coef = jnp.where(r % 2 == 0, 1.0, -lam)
    o4 = _group_rows(jnp.where(lane_head == r // 2, on * coef, 0.0), nblk)
    g = g_ref[...]
    o_ref[...] = jnp.concatenate(
        [_rms(o4[:, h * A_VD:(h + 1) * A_VD], g) * (1.0 - lam_init) for h in range(A_HEADS)], axis=1)


def _decode_fox_kernel(pt_ref, q_ref, kn_ref, vn_ref, ck_ref, *refs):
    k_refs, v_refs, o_ref = refs[:N_PAGES], refs[N_PAGES:2 * N_PAGES], refs[2 * N_PAGES]
    nblk = C_HEADS
    s = _decode_scores(q_ref, kn_ref, k_refs, nblk)
    ck = ck_ref[...]
    ck_rows = jnp.concatenate([ck] * DEC_SEQ, axis=0)
    cq = jnp.concatenate([ck[:, PAST_LEN + t:PAST_LEN + t + 1] for t in range(DEC_SEQ)], axis=0)
    on = _decode_softmax_pv(s + cq - ck_rows, vn_ref, v_refs, nblk)
    r = lax.broadcasted_iota(jnp.int32, on.shape, 0) % nblk
    lane_head = lax.broadcasted_iota(jnp.int32, on.shape, 1) // C_HD
    o_ref[...] = _group_rows(jnp.where(lane_head == r, on, 0.0), nblk)


def _decode_attn(kernel, name, layer, page_table, q, kn, vn, small, small_specs, cache_k, cache_v):
    bd, _, w = kn.shape
    row = pl.BlockSpec((None, DEC_SEQ, w), lambda b, pt: (b, 0, 0))

    def page(j):
        return pl.BlockSpec((None, None, PAGE_SIZE, w), lambda b, pt: (layer, pt[b, j], 0, 0))

    pages = [page(j) for j in range(N_PAGES)]
    return pl.pallas_call(
        kernel,
        grid_spec=pltpu.PrefetchScalarGridSpec(
            num_scalar_prefetch=1,
            grid=(bd,),
            in_specs=[row, row, row] + small_specs + pages + pages,
            out_specs=row),
        out_shape=jax.ShapeDtypeStruct((bd, DEC_SEQ, w), F32),
        compiler_params=_params(("parallel",)),
        name=name,
    )(page_table, q, kn, vn, *small, *([cache_k] * N_PAGES), *([cache_v] * N_PAGES))


def _fox_cum_decode_kernel(pt_ref, lfn_ref, *refs):
    lf_refs, ck_ref = refs[:N_PAGES], refs[N_PAGES]
    eye = (lax.broadcasted_iota(jnp.int32, (C_HEADS, C_HEADS), 0)
           == lax.broadcasted_iota(jnp.int32, (C_HEADS, C_HEADS), 1)).astype(BF16)

    def transpose_exact(x):
        out = jnp.zeros((C_HEADS, PAGE_SIZE), F32)
        for _ in range(3):
            piece = x.astype(BF16)
            out = out + _dot_nt(eye, piece)
            x = x - piece.astype(F32)
        return out

    parts = [transpose_exact(r[...]) for r in lf_refs] + [lfn_ref[...]]
    ck_ref[...] = _cumsum(jnp.concatenate(parts, axis=1), 1)


def _fox_cum_decode(layer, page_table, lfn, cache_logf):
    bd = lfn.shape[0]

    def page(j):
        return pl.BlockSpec((None, None, PAGE_SIZE, C_HEADS), lambda b, pt: (layer, pt[b, j], 0, 0))

    return pl.pallas_call(
        _fox_cum_decode_kernel,
        grid_spec=pltpu.PrefetchScalarGridSpec(
            num_scalar_prefetch=1,
            grid=(bd,),
            in_specs=[pl.BlockSpec((None, C_HEADS, PAGE_SIZE), lambda b, pt: (b, 0, 0))]
            + [page(j) for j in range(N_PAGES)],
            out_specs=pl.BlockSpec((None, C_HEADS, KEYS_PAD), lambda b, pt: (b, 0, 0))),
        out_shape=jax.ShapeDtypeStruct((bd, C_HEADS, KEYS_PAD), F32),
        compiler_params=_params(("parallel",)),
        name="fox_cum_decode",
    )(page_table, lfn, *([cache_logf] * N_PAGES))


def _pool_project(d_list, pw_ref, ps_ref):
    ps = ps_ref[...]
    outs = [_dot(d.astype(BF16), pw_ref[g]) * ps[:, g * POOL_GC:(g + 1) * POOL_GC]
            for g, d in enumerate(d_list)]
    return jnp.concatenate(outs, axis=1)


def _pool_prompt_kernel(p_ref, pw_ref, ps_ref, o_ref):
    t = p_ref.shape[0]
    row = lax.broadcasted_iota(jnp.int32, (t, POOL_GC), 0)
    d_list = []
    for g, w in enumerate(POOL_WINDOWS):
        x = p_ref[:, g * POOL_GC:(g + 1) * POOL_GC]
        acc, s = x, 1
        while s < w:
            acc = acc + _shift_down(acc, s, row)
            s *= 2
        cnt = jnp.minimum(w, row[:, 0:1] + 1).astype(F32)
        d_list.append(acc / cnt - x)
    o_ref[...] = _pool_project(d_list, pw_ref, ps_ref)


def _pool_prompt(p, pool_w, pool_scale):
    b, t, c = p.shape
    return pl.pallas_call(
        _pool_prompt_kernel,
        grid=(b,),
        in_specs=[pl.BlockSpec((None, t, c), lambda i: (i, 0, 0)),
                  pl.BlockSpec(pool_w.shape, lambda i: (0, 0, 0)),
                  pl.BlockSpec((1, c), lambda i: (0, 0))],
        out_specs=pl.BlockSpec((None, t, c), lambda i: (i, 0, 0)),
        out_shape=jax.ShapeDtypeStruct((b, t, c), F32),
        compiler_params=_params(("parallel",)),
        name="pool_prompt",
    )(p, pool_w, pool_scale)


def _pool_sample_kernel(p_ref, pre_ref, pw_ref, ps_ref, o_ref):
    for t in range(DEC_SEQ):
        d_list = []
        for g, w in enumerate(POOL_WINDOWS):
            sl = slice(g * POOL_GC, (g + 1) * POOL_GC)
            x = p_ref[t, :, sl]
            acc = x
            for j in range(1, w):
                src = t - j
                acc = acc + (p_ref[src, :, sl] if src >= 0 else pre_ref[POOL_BUF + src, :, sl])
            cnt = float(min(w, PAST_LEN + t + 1))
            d_list.append(acc / cnt - x)
        o_ref[t] = _pool_project(d_list, pw_ref, ps_ref)


def _pool_sample(p_tm, prefix_tm, pool_w, pool_scale):
    return pl.pallas_call(
        _pool_sample_kernel,
        out_shape=jax.ShapeDtypeStruct(p_tm.shape, F32),
        compiler_params=pltpu.CompilerParams(vmem_limit_bytes=VMEM_LIMIT),
        name="pool_sample",
    )(p_tm, prefix_tm, pool_w, pool_scale)


def _out_proj_kernel(nparts, x_ref, g_ref, w_ref, *refs):
    parts, o_ref = refs[:nparts], refs[nparts]
    y, off = None, 0
    for a_ref in parts:
        n = a_ref.shape[1]
        d = _dot(a_ref[...].astype(BF16), w_ref[off:off + n, :])
        y = d if y is None else y + d
        off += n
    o_ref[...] = x_ref[...] + _rms(y, g_ref[...])


def _out_proj(x, g, w, parts, tm):
    m = x.shape[0]
    tok = lambda i: (i, 0)
    return pl.pallas_call(
        functools.partial(_out_proj_kernel, len(parts)),
        grid=(m // tm,),
        in_specs=[pl.BlockSpec((tm, D_MODEL), tok),
                  pl.BlockSpec((1, D_MODEL), lambda i: (0, 0)),
                  pl.BlockSpec(w.shape, lambda i: (0, 0))]
        + [pl.BlockSpec((tm, a.shape[1]), tok) for a in parts],
        out_specs=pl.BlockSpec((tm, D_MODEL), tok),
        out_shape=jax.ShapeDtypeStruct((m, D_MODEL), F32),
        compiler_params=_params(("parallel",)),
        name="out_proj",
    )(x, g, w, *parts)


FFN_FC = 1408
FFN_NF = D_FF // FFN_FC


def _conv_gate(um2, um1, u, cw_ref, cb_ref):
    cw = cw_ref[...]
    return cb_ref[...] + cw[0:1] * um2 + cw[1:2] * um1 + cw[2:3] * u


def _ffn_prompt_kernel(x_ref, gpre_ref, gpost_ref, wg_ref, wv_ref, cwg_ref, cwv_ref, cbg_ref, cbv_ref,
                       wd_ref, o_ref, cg_ref, cv_ref, h_scr, acc_scr, carry_g, carry_v):
    it, f = pl.program_id(1), pl.program_id(2)
    tm = x_ref.shape[0]

    @pl.when(f == 0)
    def _():
        h_scr[...] = _rms(x_ref[...], gpre_ref[...]).astype(BF16)

    @pl.when(it == 0)
    def _():
        carry_g[f] = jnp.zeros((8, FFN_FC), F32)
        carry_v[f] = jnp.zeros((8, FFN_FC), F32)

    row = lax.broadcasted_iota(jnp.int32, (tm, FFN_FC), 0)

    def conv(w_ref, cw_ref, cb_ref, carry, tail_ref):
        u = _dot(h_scr[...], w_ref[...])
        prev = carry[f]
        p1, p2 = prev[7:8], prev[6:7]
        um1 = jnp.where(row >= 1, pltpu.roll(u, 1, 0), p1)
        um2 = jnp.where(row >= 2, pltpu.roll(u, 2, 0), jnp.where(row == 0, p2, p1))
        carry[f] = u[tm - 8:]
        tail_ref[...] = u[tm - (CONV_W - 1):]
        return _conv_gate(um2, um1, u, cw_ref, cb_ref)

    cg = conv(wg_ref, cwg_ref, cbg_ref, carry_g, cg_ref)
    cv = conv(wv_ref, cwv_ref, cbv_ref, carry_v, cv_ref)
    act = (cg * jax.nn.sigmoid(cg) * cv).astype(BF16)
    y = _dot(act, wd_ref[...])

    @pl.when(f == 0)
    def _():
        acc_scr[...] = y

    @pl.when(f > 0)
    def _():
        acc_scr[...] += y

    @pl.when(f == FFN_NF - 1)
    def _():
        o_ref[...] = x_ref[...] + _rms(acc_scr[...], gpost_ref[...])


def _ffn_prompt(x, gpre, gpost, w_up, conv_w, conv_b, w_down, tm=512):
    b, t, _ = x.shape
    fc, nf = FFN_FC, FFN_NF
    tok = lambda bi, it, f: (bi, it, 0)
    const = lambda bi, it, f: (0, 0)
    gcol = lambda bi, it, f: (0, f)
    vcol = lambda bi, it, f: (0, nf + f)
    tail = pl.BlockSpec((None, None, CONV_W - 1, fc), lambda bi, it, f: (bi, it, 0, f))
    return pl.pallas_call(
        _ffn_prompt_kernel,
        grid=(b, t // tm, nf),
        in_specs=[pl.BlockSpec((None, tm, D_MODEL), tok),
                  pl.BlockSpec((1, D_MODEL), const), pl.BlockSpec((1, D_MODEL), const),
                  pl.BlockSpec((D_MODEL, fc), gcol), pl.BlockSpec((D_MODEL, fc), vcol),
                  pl.BlockSpec((CONV_W, fc), gcol), pl.BlockSpec((CONV_W, fc), vcol),
                  pl.BlockSpec((1, fc), gcol), pl.BlockSpec((1, fc), vcol),
                  pl.BlockSpec((fc, D_MODEL), lambda bi, it, f: (f, 0))],
        out_specs=[pl.BlockSpec((None, tm, D_MODEL), tok), tail, tail],
        out_shape=[jax.ShapeDtypeStruct((b, t, D_MODEL), F32),
                   jax.ShapeDtypeStruct((b, t // tm, CONV_W - 1, D_FF), F32),
                   jax.ShapeDtypeStruct((b, t // tm, CONV_W - 1, D_FF), F32)],
        scratch_shapes=[pltpu.VMEM((tm, D_MODEL), BF16), pltpu.VMEM((tm, D_MODEL), F32),
                        pltpu.VMEM((nf, 8, fc), F32), pltpu.VMEM((nf, 8, fc), F32)],
        compiler_params=_params(("arbitrary", "arbitrary", "arbitrary")),
        name="ffn_prompt",
    )(x, gpre, gpost, w_up, w_up, conv_w, conv_w, conv_b, conv_b, w_down)


def _ffn_sample_kernel(x_ref, gpre_ref, gpost_ref, wg_ref, wv_ref, cwg_ref, cwv_ref, cbg_ref, cbv_ref,
                       wd_ref, pg_ref, pv_ref, o_ref, cg_ref, cv_ref, h_scr, acc_scr):
    f = pl.program_id(0)
    m = x_ref.shape[0]
    bd = m // DEC_SEQ

    @pl.when(f == 0)
    def _():
        h_scr[...] = _rms(x_ref[...], gpre_ref[...]).astype(BF16)

    def conv(w_ref, cw_ref, cb_ref, pre_ref, tail_ref):
        u = _dot(h_scr[...], w_ref[...])
        full = jnp.concatenate([pre_ref[...], u], axis=0)
        tail_ref[...] = u[m - (CONV_W - 1) * bd:]
        return _conv_gate(full[:m], full[bd:bd + m], u, cw_ref, cb_ref)

    cg = conv(wg_ref, cwg_ref, cbg_ref, pg_ref, cg_ref)
    cv = conv(wv_ref, cwv_ref, cbv_ref, pv_ref, cv_ref)
    act = (cg * jax.nn.sigmoid(cg) * cv).astype(BF16)
    y = _dot(act, wd_ref[...])

    @pl.when(f == 0)
    def _():
        acc_scr[...] = y

    @pl.when(f > 0)
    def _():
        acc_scr[...] += y

    @pl.when(f == FFN_NF - 1)
    def _():
        o_ref[...] = x_ref[...] + _rms(acc_scr[...], gpost_ref[...])


def _ffn_sample(x, gpre, gpost, w_up, conv_w, conv_b, w_down, prefix_tm):
    m = x.shape[0]
    bd = m // DEC_SEQ
    fc, nf = FFN_FC, FFN_NF
    const = lambda f: (0, 0)
    gcol = lambda f: (0, f)
    vcol = lambda f: (0, nf + f)
    tail = pl.BlockSpec(((CONV_W - 1) * bd, fc), gcol)
    return pl.pallas_call(
        _ffn_sample_kernel,
        grid=(nf,),
        in_specs=[pl.BlockSpec((m, D_MODEL), const),
                  pl.BlockSpec((1, D_MODEL), const), pl.BlockSpec((1, D_MODEL), const),
                  pl.BlockSpec((D_MODEL, fc), gcol), pl.BlockSpec((D_MODEL, fc), vcol),
                  pl.BlockSpec((CONV_W, fc), gcol), pl.BlockSpec((CONV_W, fc), vcol),
                  pl.BlockSpec((1, fc), gcol), pl.BlockSpec((1, fc), vcol),
                  pl.BlockSpec((fc, D_MODEL), lambda f: (f, 0)),
                  pl.BlockSpec(((CONV_W - 1) * bd, fc), gcol),
                  pl.BlockSpec(((CONV_W - 1) * bd, fc), vcol)],
        out_specs=[pl.BlockSpec((m, D_MODEL), const), tail, tail],
        out_shape=[jax.ShapeDtypeStruct((m, D_MODEL), F32),
                   jax.ShapeDtypeStruct(((CONV_W - 1) * bd, D_FF), F32),
                   jax.ShapeDtypeStruct(((CONV_W - 1) * bd, D_FF), F32)],
        scratch_shapes=[pltpu.VMEM((m, D_MODEL), BF16), pltpu.VMEM((m, D_MODEL), F32)],
        compiler_params=_params(("arbitrary",)),
        name="ffn_sample",
    )(x, gpre, gpost, w_up, w_up, conv_w, conv_w, conv_b, conv_b, w_down, prefix_tm, prefix_tm)


def _rope_tables(pos):
    half = ROT_DIM // 2
    inv = ROPE_THETA ** (-jnp.arange(0, ROT_DIM, 2, dtype=F32) / ROT_DIM)
    ang = pos.astype(F32)[:, None] * inv[None, :]
    cos, sin = jnp.cos(ang), jnp.sin(ang)
    n = pos.shape[0]
    pad = jnp.zeros((n, A_HD - ROT_DIM), F32)
    zero = jnp.zeros((n, half), F32)
    c = jnp.concatenate([cos, cos, pad + 1.0], axis=1)
    s1 = jnp.concatenate([zero, sin, pad], axis=1)
    s2 = jnp.concatenate([-sin, zero, pad], axis=1)
    return tuple(jnp.concatenate([a, a], axis=1) for a in (c, s1, s2))


def _to_bm(a_tm):
    return a_tm.reshape(DEC_SEQ, DEC_BATCH, -1).transpose(1, 0, 2)


def _to_tm(a_bm):
    return a_bm.transpose(1, 0, 2).reshape(DEC_SEQ * DEC_BATCH, -1)


def kernel(x_prompt, x_sample, cache_diff_k, cache_diff_v, state_pool, cache_fox_k, cache_fox_v, cache_fox_logf, state_ffn_conv, page_table, w_in_ab, diff_lambda, diff_subln_g, pool_w, pool_scale, w_out_ab, w_in_c, b_f, w_out_c, norm_mix_pre, norm_mix_post, norm_ffn_pre, norm_ffn_post, w_up, conv_w, conv_b, w_down):
    n_phys = cache_diff_k.shape[1]
    ck_diff = cache_diff_k.reshape(-1, n_phys, PAGE_SIZE, A_QK_W)
    cv_diff = cache_diff_v.reshape(-1, n_phys, PAGE_SIZE, A_V_W)
    ck_fox = cache_fox_k.reshape(-1, n_phys, PAGE_SIZE, C_W)
    cv_fox = cache_fox_v.reshape(-1, n_phys, PAGE_SIZE, C_W)

    tabs_p = _rope_tables(jnp.arange(SEQ))
    tabs_s = _rope_tables(PAST_LEN + jnp.repeat(jnp.arange(DEC_SEQ), DEC_BATCH))

    xp = x_prompt.reshape(BATCH * SEQ, D_MODEL)
    xs = x_sample.transpose(1, 0, 2).reshape(DEC_SEQ * DEC_BATCH, D_MODEL)
    tm_p, tm_s = 512, DEC_SEQ * DEC_BATCH
    row = lambda a: a.reshape(1, -1)
    out = {n: [] for n in ("dk_p", "dv_p", "pl_p", "fk_p", "fv_p", "fl_p", "cv_p",
                           "dk_s", "dv_s", "pl_s", "fk_s", "fv_s", "fl_s", "cv_s")}

    for l in range(DEPTH):
        i = l // 2
        if l % 2 == 0:
            lam_init = 0.8 - 0.6 * math.exp(-0.3 * l)
            w_in = w_in_ab[i].astype(BF16)
            w_out = w_out_ab[i].astype(BF16)
            pw = pool_w[i].astype(BF16)
            small = [diff_lambda[i], row(diff_subln_g[i])]
            q, k, v, p = _inproj_even(xp, row(norm_mix_pre[l]), w_in, tabs_p, tm_p)
            b3 = lambda a: a.reshape(BATCH, SEQ, -1)
            o = _attn_prompt(False, lam_init, b3(q), b3(k), b3(v), small)
            pooled = _pool_prompt(b3(p), pw, row(pool_scale[i]))
            xp = _out_proj(xp, row(norm_mix_post[l]), w_out,
                           [o.reshape(-1, A_V_W), pooled.reshape(-1, POOL_W)], tm_p)
            out["dk_p"].append(k.reshape(BATCH, SEQ, A_HEADS, 2 * A_HD))
            out["dv_p"].append(v.reshape(BATCH, SEQ, A_HEADS, A_VD))
            out["pl_p"].append(b3(p)[:, SEQ - POOL_BUF:])
            q, k, v, p = _inproj_even(xs, row(norm_mix_pre[l]), w_in, tabs_s, tm_s)
            kb, vb = _to_bm(k), _to_bm(v)
            const = lambda b, pt: (0, 0)
            o = _decode_attn(functools.partial(_decode_diff_kernel, lam_init), "diff_attn_decode", i,
                             page_table, _to_bm(q).astype(F32), kb, vb, small,
                             [pl.BlockSpec((4, A_HD), const), pl.BlockSpec((1, A_VD), const)],
                             ck_diff, cv_diff)
            pooled = _pool_sample(p.reshape(DEC_SEQ, DEC_BATCH, POOL_W),
                                  state_pool[i].transpose(1, 0, 2), pw, row(pool_scale[i]))
            xs = _out_proj(xs, row(norm_mix_post[l]), w_out,
                           [_to_tm(o), pooled.reshape(-1, POOL_W)], tm_s)
            out["dk_s"].append(kb.reshape(DEC_BATCH, DEC_SEQ, A_HEADS, 2 * A_HD))
            out["dv_s"].append(vb.reshape(DEC_BATCH, DEC_SEQ, A_HEADS, A_VD))
            out["pl_s"].append(jnp.concatenate([state_pool[i], _to_bm(p)], axis=1)[:, -POOL_BUF:])
        else:
            w_qkv = w_in_c[i][:, :3 * C_W].astype(BF16)
            w_f = w_in_c[i][:, 3 * C_W:].astype(BF16)
            w_out = w_out_c[i].astype(BF16)
            q, k, v, lf, lft = _inproj_odd(xp, row(norm_mix_pre[l]), w_qkv, w_f, w_f.T, b_f[i], tm_p)
            b3 = lambda a: a.reshape(BATCH, SEQ, -1)
            cum, cumt = _fox_cum_prompt(b3(lf), lft)
            tq = 512
            cq = cum.reshape(BATCH, SEQ, C_HEADS // 2, 2).transpose(0, 2, 1, 3)
            ck = cumt.reshape(BATCH, C_HEADS // 2, 2, SEQ // tq, tq).transpose(0, 1, 3, 2, 4)
            o = _attn_prompt(True, 0.0, b3(q), b3(k), b3(v), [cq, ck], tq)
            xp = _out_proj(xp, row(norm_mix_post[l]), w_out, [o.reshape(-1, C_W)], tm_p)
            out["fk_p"].append(k.reshape(BATCH, SEQ, C_HEADS, C_HD))
            out["fv_p"].append(v.reshape(BATCH, SEQ, C_HEADS, C_HD))
            out["fl_p"].append(b3(lf))
            q, k, v, lf, lft = _inproj_odd(xs, row(norm_mix_pre[l]), w_qkv, w_f, w_f.T, b_f[i], tm_s)
            kb, vb = _to_bm(k), _to_bm(v)
            lfn = lft.reshape(C_HEADS, DEC_SEQ, DEC_BATCH).transpose(2, 0, 1)
            lfn = jnp.pad(lfn, ((0, 0), (0, 0), (0, PAGE_SIZE - DEC_SEQ)))
            ckd = _fox_cum_decode(i, page_table, lfn, cache_fox_logf)
            o = _decode_attn(_decode_fox_kernel, "fox_attn_decode", i, page_table, _to_bm(q).astype(F32), kb, vb,
                             [ckd], [pl.BlockSpec((None, C_HEADS, KEYS_PAD), lambda b, pt: (b, 0, 0))],
                             ck_fox, cv_fox)
            xs = _out_proj(xs, row(norm_mix_post[l]), w_out, [_to_tm(o)], tm_s)
            out["fk_s"].append(kb.reshape(DEC_BATCH, DEC_SEQ, C_HEADS, C_HD))
            out["fv_s"].append(vb.reshape(DEC_BATCH, DEC_SEQ, C_HEADS, C_HD))
            out["fl_s"].append(_to_bm(lf))

        wu, wd = w_up[l].astype(BF16), w_down[l].astype(BF16)
        ffn_args = (row(norm_ffn_pre[l]), row(norm_ffn_post[l]), wu, conv_w[l], row(conv_b[l]), wd)
        xp3, tg, tv = _ffn_prompt(xp.reshape(BATCH, SEQ, D_MODEL), *ffn_args)
        xp = xp3.reshape(BATCH * SEQ, D_MODEL)
        out["cv_p"].append(jnp.concatenate([tg[:, -1], tv[:, -1]], axis=-1))
        prefix_tm = state_ffn_conv[l].transpose(1, 0, 2).reshape((CONV_W - 1) * DEC_BATCH, 2 * D_FF)
        xs, tg, tv = _ffn_sample(xs, *ffn_args, prefix_tm)
        tail = jnp.concatenate([tg, tv], axis=-1).reshape(CONV_W - 1, DEC_BATCH, 2 * D_FF)
        out["cv_s"].append(tail.transpose(1, 0, 2))

    st = {n: jnp.stack(v) for n, v in out.items()}
    y_p = xp.reshape(BATCH, SEQ, D_MODEL)
    y_s = xs.reshape(DEC_SEQ, DEC_BATCH, D_MODEL).transpose(1, 0, 2)
    return (y_p, y_s, st["dk_p"], st["dv_p"], st["pl_p"], st["fk_p"], st["fv_p"], st["fl_p"], st["cv_p"],
            st["dk_s"], st["dv_s"], st["pl_s"], st["fk_s"], st["fv_s"], st["fl_s"], st["cv_s"])
```

```python
import functools
import math

import jax
import jax.numpy as jnp
from jax import lax
from jax.experimental import pallas as pl
from jax.experimental.pallas import tpu as pltpu

D_MODEL = 1024
BATCH = 8
SEQ = 2048
DEPTH = 4
DEC_BATCH = 128
DEC_SEQ = 4
PAST_LEN = 2048
PAGE_SIZE = 128
N_PAGES = PAST_LEN // PAGE_SIZE
A_HEADS = 4
A_HD = 64
A_VD = 128
A_QK_W = 512
A_V_W = 512
POOL_WINDOWS = (2, 4, 8, 16)
POOL_W = 512
POOL_GC = 128
POOL_BUF = 15
C_HD = 64
C_HEADS = 16
C_W = 1024
ROT_DIM = 16
ROPE_THETA = 500000.0
D_FF = 2816
CONV_W = 3
EPS = 1e-6
NEG_INF = -1e30

LANES = 128
VMEM_LIMIT = 56 * 1024 * 1024

F32 = jnp.float32
BF16 = jnp.bfloat16
NT_DIMS = (((1,), (1,)), ((), ()))


def _params(sem, vmem=VMEM_LIMIT):
    return pltpu.CompilerParams(dimension_semantics=sem, vmem_limit_bytes=vmem)


def _rms(x, g):
    return x * lax.rsqrt(jnp.mean(x * x, axis=-1, keepdims=True) + EPS) * g


def _dot(a, b):
    return jnp.dot(a, b, preferred_element_type=F32)


def _dot_nt(a, b):
    return lax.dot_general(a, b, NT_DIMS, preferred_element_type=F32)


def _log_sigmoid(x):
    return jnp.minimum(x, 0.0) - jnp.log1p(jnp.exp(-jnp.abs(x)))


def _shift_down(x, s, row):
    return jnp.where(row >= s, pltpu.roll(x, s, 0), 0.0)


def _cumsum(x, axis):
    n = x.shape[axis]
    idx = lax.broadcasted_iota(jnp.int32, x.shape, axis)
    s = 1
    while s < n:
        x = x + jnp.where(idx >= s, pltpu.roll(x, s, axis), 0.0)
        s *= 2
    return x


def _rope(z, c_ref, s1_ref, s2_ref):
    outs = []
    for c in range(z.shape[1] // LANES):
        zc = z[:, c * LANES:(c + 1) * LANES]
        outs.append(zc * c_ref[...] + pltpu.roll(zc, ROT_DIM // 2, 1) * s1_ref[...]
                    + pltpu.roll(zc, LANES - ROT_DIM // 2, 1) * s2_ref[...])
    return jnp.concatenate(outs, axis=1)


def _inproj_even_kernel(x_ref, g_ref, w_ref, c_ref, s1_ref, s2_ref,
                        q_ref, k_ref, v_ref, p_ref, h_scr):
    j = pl.program_id(1)

    @pl.when(j == 0)
    def _():
        h_scr[...] = _rms(x_ref[...], g_ref[...]).astype(BF16)

    z = _dot(h_scr[...], w_ref[...])

    @pl.when(j == 0)
    def _():
        q_ref[...] = (_rope(z, c_ref, s1_ref, s2_ref) * (A_HD ** -0.5)).astype(BF16)

    @pl.when(j == 1)
    def _():
        k_ref[...] = _rope(z, c_ref, s1_ref, s2_ref)

    @pl.when(j == 2)
    def _():
        v_ref[...] = z

    @pl.when(j == 3)
    def _():
        p_ref[...] = z


def _inproj_even(x, g, w, tabs, tm):
    m = x.shape[0]
    ntab = tabs[0].shape[0] // tm
    tok = lambda i, j: (i, 0)
    tab = pl.BlockSpec((tm, LANES), lambda i, j: (i % ntab, 0))
    out = pl.BlockSpec((tm, A_QK_W), tok)
    return pl.pallas_call(
        _inproj_even_kernel,
        grid=(m // tm, 4),
        in_specs=[pl.BlockSpec((tm, D_MODEL), tok),
                  pl.BlockSpec((1, D_MODEL), lambda i, j: (0, 0)),
                  pl.BlockSpec((D_MODEL, A_QK_W), lambda i, j: (0, j)),
                  tab, tab, tab],
        out_specs=[out, out, out, out],
        out_shape=[jax.ShapeDtypeStruct((m, A_QK_W), BF16)]
        + [jax.ShapeDtypeStruct((m, A_QK_W), F32)] * 3,
        scratch_shapes=[pltpu.VMEM((tm, D_MODEL), BF16)],
        compiler_params=_params(("parallel", "arbitrary")),
        name="inproj_even",
    )(x, g, w, *tabs)


def _inproj_odd_kernel(x_ref, g_ref, w_ref, wf_ref, wft_ref, bf_ref, bft_ref,
                       q_ref, k_ref, v_ref, lf_ref, lft_ref, h_scr):
    j = pl.program_id(1)

    @pl.when(j == 0)
    def _():
        h = _rms(x_ref[...], g_ref[...]).astype(BF16)
        h_scr[...] = h
        lf_ref[...] = _log_sigmoid(_dot(h, wf_ref[...]) + bf_ref[...])
        lft_ref[...] = _log_sigmoid(_dot_nt(wft_ref[...], h) + bft_ref[...])

    z = _dot(h_scr[...], w_ref[...])

    @pl.when(j == 0)
    def _():
        q_ref[...] = (z * (C_HD ** -0.5)).astype(BF16)

    @pl.when(j == 1)
    def _():
        k_ref[...] = z

    @pl.when(j == 2)
    def _():
        v_ref[...] = z


def _inproj_odd(x, g, w_qkv, w_f, w_ft, b_f, tm):
    m = x.shape[0]
    tok = lambda i, j: (i, 0)
    const = lambda i, j: (0, 0)
    out = pl.BlockSpec((tm, C_W), tok)
    return pl.pallas_call(
        _inproj_odd_kernel,
        grid=(m // tm, 3),
        in_specs=[pl.BlockSpec((tm, D_MODEL), tok),
                  pl.BlockSpec((1, D_MODEL), const),
                  pl.BlockSpec((D_MODEL, C_W), lambda i, j: (0, j)),
                  pl.BlockSpec((D_MODEL, C_HEADS), const),
                  pl.BlockSpec((C_HEADS, D_MODEL), const),
                  pl.BlockSpec((1, C_HEADS), const),
                  pl.BlockSpec((C_HEADS, 1), const)],
        out_specs=[out, out, out,
                   pl.BlockSpec((tm, C_HEADS), tok),
                   pl.BlockSpec((C_HEADS, tm), lambda i, j: (0, i))],
        out_shape=[jax.ShapeDtypeStruct((m, C_W), BF16),
                   jax.ShapeDtypeStruct((m, C_W), F32),
                   jax.ShapeDtypeStruct((m, C_W), F32),
                   jax.ShapeDtypeStruct((m, C_HEADS), F32),
                   jax.ShapeDtypeStruct((C_HEADS, m), F32)],
        scratch_shapes=[pltpu.VMEM((tm, D_MODEL), BF16)],
        compiler_params=_params(("parallel", "arbitrary")),
        name="inproj_odd",
    )(x, g, w_qkv, w_f, w_ft, b_f.reshape(1, C_HEADS), b_f.reshape(C_HEADS, 1))


def _diff_lambda(dl_ref, lam_init):
    dl = dl_ref[...]
    a = jnp.sum(dl[0:1] * dl[1:2], axis=1, keepdims=True)
    b = jnp.sum(dl[2:3] * dl[3:4], axis=1, keepdims=True)
    return jnp.exp(a) - jnp.exp(b) + lam_init


ATTN_RG = 128
ATTN_RS = 16


def _attn_prompt_kernel(fox, lam_init, tq, tk, *refs):
    if fox:
        (q_ref, k_ref, v_ref, cq_ref, ck_ref, o_ref,
         kb_scr, vb_scr, q2_scr, m_scr, l_scr, acc_scr, cq_scr) = refs
    else:
        (q_ref, k_ref, v_ref, dl_ref, g_ref, o_ref,
         kb_scr, vb_scr, q2_scr, m_scr, l_scr, acc_scr) = refs
    iq = pl.program_id(2)
    rg, rs = ATTN_RG, ATTN_RS
    reps = tk // LANES

    @pl.when(iq == 0)
    def _():
        kb_scr[...] = k_ref[...].astype(BF16)
        vb_scr[...] = v_ref[...].astype(BF16)

    q = q_ref[...]
    lane = lax.broadcasted_iota(jnp.int32, q.shape, 1)
    zero = jnp.zeros_like(q)
    q2_scr[0:tq] = jnp.where(lane < A_HD, q, zero)
    q2_scr[tq:2 * tq] = jnp.where(lane >= A_HD, q, zero)
    m_scr[...] = jnp.full(m_scr.shape, NEG_INF, F32)
    l_scr[...] = jnp.zeros(l_scr.shape, F32)
    acc_scr[...] = jnp.zeros(acc_scr.shape, F32)
    if fox:
        cq = cq_ref[...]
        cq_scr[0:tq] = jnp.broadcast_to(cq[:, 0:1], (tq, LANES))
        cq_scr[tq:2 * tq] = jnp.broadcast_to(cq[:, 1:2], (tq, LANES))

    def wide(x):
        return jnp.concatenate([x] * reps, axis=1)

    def block(ik, masked):
        start = pl.multiple_of(ik * tk, tk)

        s = _dot_nt(q2_scr[...], kb_scr[pl.ds(start, tk), :])
        if masked:
            col = lax.broadcasted_iota(jnp.int32, (rs, tk), 1)
            row = lax.broadcasted_iota(jnp.int32, (rs, tk), 0)
        p_parts, a_parts = [], []
        for j in range(2 * tq // rs):
            rows = slice(j * rs, (j + 1) * rs)
            sj = s[rows]
            if fox:
                ck_row = ck_ref[pl.ds(2 * ik + j * rs // tq, 1), :]
                sj = sj + wide(cq_scr[rows, :]) - ck_row
            if masked:
                sj = jnp.where(col <= row + (j * rs) % tq, sj, NEG_INF)
            m_old = m_scr[rows, :]
            m_new = jnp.maximum(m_old, jnp.max(sj, axis=1, keepdims=True))
            pj = jnp.exp(sj - wide(m_new))
            alpha = jnp.exp(m_old - m_new)
            l_scr[rows, :] = alpha * l_scr[rows, :] + jnp.sum(pj, axis=1, keepdims=True)
            m_scr[rows, :] = m_new
            p_parts.append(pj.astype(BF16))
            a_parts.append(alpha)
        p = jnp.concatenate(p_parts, axis=0)
        alpha = jnp.concatenate(a_parts, axis=0)
        acc_scr[...] = alpha * acc_scr[...] + _dot(p, vb_scr[pl.ds(start, tk), :])

    def full_block(ik, carry):
        block(ik, False)
        return carry

    lax.fori_loop(0, iq, full_block, 0)
    block(iq, True)
    on = acc_scr[...] / l_scr[...]
    top, bot = on[:tq], on[tq:]
    if fox:
        o_ref[...] = jnp.where(lane < C_HD, top, bot)
    else:
        o = top - _diff_lambda(dl_ref, lam_init) * bot
        o_ref[...] = _rms(o, g_ref[...]) * (1.0 - lam_init)


def _attn_prompt(fox, lam_init, q, k, v, extra, tq=512):
    b, t, w = k.shape
    ng, nq = w // LANES, t // tq
    tile = lambda bi, g, iq: (bi, iq, g)
    seq = lambda bi, g, iq: (bi, 0, g)
    stat = pltpu.VMEM((2 * tq, LANES), F32)
    scratch = [pltpu.VMEM((t, LANES), BF16), pltpu.VMEM((t, LANES), BF16),
               pltpu.VMEM((2 * tq, LANES), BF16), stat, stat, stat]
    if fox:
        cq, ck = extra
        extra_specs = [pl.BlockSpec((None, None, tq, 2), lambda bi, g, iq: (bi, g, iq, 0)),
                       pl.BlockSpec((None, None, 2 * nq, tq), lambda bi, g, iq: (bi, g, 0, 0))]
        scratch.append(stat)
    else:
        extra_specs = [pl.BlockSpec((4, A_HD), lambda bi, g, iq: (0, 0)),
                       pl.BlockSpec((1, A_VD), lambda bi, g, iq: (0, 0))]
    return pl.pallas_call(
        functools.partial(_attn_prompt_kernel, fox, lam_init, tq, tq),
        grid=(b, ng, nq),
        in_specs=[pl.BlockSpec((None, tq, LANES), tile),
                  pl.BlockSpec((None, t, LANES), seq),
                  pl.BlockSpec((None, t, LANES), seq)] + extra_specs,
        out_specs=pl.BlockSpec((None, tq, LANES), tile),
        out_shape=jax.ShapeDtypeStruct((b, t, w), F32),
        scratch_shapes=scratch,
        compiler_params=_params(("parallel", "parallel", "arbitrary")),
        name="fox_attn_prompt" if fox else "diff_attn_prompt",
    )(q, k, v, *extra)


def _fox_cum_prompt_kernel(lf_ref, lft_ref, cum_ref, cumt_ref):
    cum_ref[...] = _cumsum(lf_ref[...], 0)
    cumt_ref[...] = _cumsum(lft_ref[...], 1)


def _fox_cum_prompt(lf, lft):
    b, t, h = lf.shape
    return pl.pallas_call(
        _fox_cum_prompt_kernel,
        grid=(b,),
        in_specs=[pl.BlockSpec((None, t, h), lambda i: (i, 0, 0)),
                  pl.BlockSpec((h, t), lambda i: (0, i))],
        out_specs=[pl.BlockSpec((None, t, h), lambda i: (i, 0, 0)),
                   pl.BlockSpec((None, h, t), lambda i: (i, 0, 0))],
        out_shape=[jax.ShapeDtypeStruct((b, t, h), F32),
                   jax.ShapeDtypeStruct((b, h, t), F32)],
        compiler_params=_params(("parallel",)),
        name="fox_cum_prompt",
    )(lf, lft)


KEYS_PAD = PAST_LEN + PAGE_SIZE
DIFF_PAGE_ROWS = PAGE_SIZE * A_HEADS
DIFF_NEW_ROWS = DEC_SEQ * A_HEADS


def _softmax_rows(s):
    m = jnp.max(s, axis=1, keepdims=True)
    p = jnp.exp(s - m)
    return p.astype(BF16), jnp.sum(p, axis=1, keepdims=True)


def _decode_diff_kernel(lam_init, pt_ref, q_ref, kn_ref, vn_ref, dl_ref, g_ref, *refs):
    k_refs, v_refs, o_ref = refs[:N_PAGES], refs[N_PAGES:2 * N_PAGES], refs[2 * N_PAGES]
    q = q_ref[...]
    lane = lax.broadcasted_iota(jnp.int32, q.shape, 1)
    q2 = jnp.concatenate([jnp.where(lane < A_HD, q, 0.0),
                          jnp.where(lane >= A_HD, q, 0.0)], axis=0).astype(BF16)
    pad = jnp.zeros((DIFF_PAGE_ROWS - DIFF_NEW_ROWS, LANES), F32)
    kn = jnp.concatenate([kn_ref[...], pad], axis=0).astype(BF16)
    vn = jnp.concatenate([vn_ref[...], pad], axis=0).astype(BF16)
    parts = [_dot_nt(q2, kr[...].astype(BF16)) for kr in k_refs]
    parts.append(_dot_nt(q2, kn))
    s = jnp.concatenate(parts, axis=1)
    r = lax.broadcasted_iota(jnp.int32, s.shape, 0)
    col = lax.broadcasted_iota(jnp.int32, s.shape, 1)
    t_row = (r % DIFF_NEW_ROWS) // A_HEADS
    valid = (col % A_HEADS == r % A_HEADS) & (col // A_HEADS <= PAST_LEN + t_row)
    pb, l = _softmax_rows(jnp.where(valid, s, NEG_INF))
    o = _dot(pb[:, N_PAGES * DIFF_PAGE_ROWS:], vn)
    for j, vr in enumerate(v_refs):
        o = o + _dot(pb[:, j * DIFF_PAGE_ROWS:(j + 1) * DIFF_PAGE_ROWS], vr[...].astype(BF16))
    on = o / l
    out = on[:DIFF_NEW_ROWS] - _diff_lambda(dl_ref, lam_init) * on[DIFF_NEW_ROWS:]
    o_ref[...] = _rms(out, g_ref[...]) * (1.0 - lam_init)


def _decode_fox_kernel(pt_ref, q_ref, kn_ref, vn_ref, lfn_ref, *refs):
    kt_refs, vt_refs = refs[:N_PAGES], refs[N_PAGES:2 * N_PAGES]
    lf_refs, o_ref = refs[2 * N_PAGES:3 * N_PAGES], refs[3 * N_PAGES]
    nblk = C_HEADS
    rows = DEC_SEQ * nblk
    q4 = q_ref[...]
    qb = jnp.concatenate([jnp.broadcast_to(q4[t:t + 1], (nblk, C_W)) for t in range(DEC_SEQ)], axis=0)
    rblk = lax.broadcasted_iota(jnp.int32, (rows, C_W), 0) % nblk
    lblk = lax.broadcasted_iota(jnp.int32, (rows, C_W), 1) // C_HD
    own = rblk == lblk
    q2 = jnp.where(own, qb, 0.0).astype(BF16)
    pad = jnp.zeros((PAGE_SIZE - DEC_SEQ, C_W), F32)
    kn = jnp.concatenate([kn_ref[...], pad], axis=0).astype(BF16)
    vn = jnp.concatenate([vn_ref[...], pad], axis=0).astype(BF16)
    parts = [_dot(q2, kr[...].astype(BF16)) for kr in kt_refs]
    parts.append(_dot_nt(q2, kn))
    s = jnp.concatenate(parts, axis=1)
    ck = _cumsum(jnp.concatenate([r[...] for r in lf_refs] + [lfn_ref[...]], axis=1), 1)
    ck_rows = jnp.concatenate([ck] * DEC_SEQ, axis=0)
    cq = jnp.concatenate([ck[:, PAST_LEN + t:PAST_LEN + t + 1] for t in range(DEC_SEQ)], axis=0)
    s = s + cq - ck_rows
    t_row = lax.broadcasted_iota(jnp.int32, s.shape, 0) // nblk
    col = lax.broadcasted_iota(jnp.int32, s.shape, 1)
    pb, l = _softmax_rows(jnp.where(col <= PAST_LEN + t_row, s, NEG_INF))
    o = _dot(pb[:, PAST_LEN:], vn)
    for j, vr in enumerate(vt_refs):
        o = o + _dot_nt(pb[:, j * PAGE_SIZE:(j + 1) * PAGE_SIZE], vr[...].astype(BF16))
    on = jnp.where(own, o / l, 0.0)
    o_ref[...] = jnp.concatenate([jnp.sum(on[t * nblk:(t + 1) * nblk], axis=0, keepdims=True)
                                  for t in range(DEC_SEQ)], axis=0)


def _decode_attn(kernel, name, layer, page_table, per_batch, shared, paged, out_rows):
    bd = per_batch[0].shape[0]
    batch_spec = lambda a: pl.BlockSpec((None,) + a.shape[1:], lambda b, pt: (b, 0, 0))
    shared_spec = lambda a: pl.BlockSpec(a.shape, lambda b, pt: (0, 0))

    def page_spec(a, j):
        return pl.BlockSpec((None, None) + a.shape[2:], lambda b, pt: (layer, pt[b, j], 0, 0))

    page_specs = [page_spec(a, j) for a in paged for j in range(N_PAGES)]
    page_args = [a for a in paged for _ in range(N_PAGES)]
    out_cols = per_batch[0].shape[2]
    return pl.pallas_call(
        kernel,
        grid_spec=pltpu.PrefetchScalarGridSpec(
            num_scalar_prefetch=1,
            grid=(bd,),
            in_specs=[batch_spec(a) for a in per_batch] + [shared_spec(a) for a in shared] + page_specs,
            out_specs=pl.BlockSpec((None, out_rows, out_cols), lambda b, pt: (b, 0, 0))),
        out_shape=jax.ShapeDtypeStruct((bd, out_rows, out_cols), F32),
        compiler_params=_params(("parallel",)),
        name=name,
    )(page_table, *per_batch, *shared, *page_args)


def _pool_project(d_list, pw_ref, ps_ref):
    ps = ps_ref[...]
    outs = [_dot(d.astype(BF16), pw_ref[g]) * ps[:, g * POOL_GC:(g + 1) * POOL_GC]
            for g, d in enumerate(d_list)]
    return jnp.concatenate(outs, axis=1)


def _pool_prompt_kernel(p_ref, pw_ref, ps_ref, o_ref):
    t = p_ref.shape[0]
    row = lax.broadcasted_iota(jnp.int32, (t, POOL_GC), 0)
    d_list = []
    for g, w in enumerate(POOL_WINDOWS):
        x = p_ref[:, g * POOL_GC:(g + 1) * POOL_GC]
        acc, s = x, 1
        while s < w:
            acc = acc + _shift_down(acc, s, row)
            s *= 2
        cnt = jnp.minimum(w, row[:, 0:1] + 1).astype(F32)
        d_list.append(acc / cnt - x)
    o_ref[...] = _pool_project(d_list, pw_ref, ps_ref)


def _pool_prompt(p, pool_w, pool_scale):
    b, t, c = p.shape
    return pl.pallas_call(
        _pool_prompt_kernel,
        grid=(b,),
        in_specs=[pl.BlockSpec((None, t, c), lambda i: (i, 0, 0)),
                  pl.BlockSpec(pool_w.shape, lambda i: (0, 0, 0)),
                  pl.BlockSpec((1, c), lambda i: (0, 0))],
        out_specs=pl.BlockSpec((None, t, c), lambda i: (i, 0, 0)),
        out_shape=jax.ShapeDtypeStruct((b, t, c), F32),
        compiler_params=_params(("parallel",)),
        name="pool_prompt",
    )(p, pool_w, pool_scale)


def _pool_sample_kernel(p_ref, pre_ref, pw_ref, ps_ref, o_ref):
    for t in range(DEC_SEQ):
        d_list = []
        for g, w in enumerate(POOL_WINDOWS):
            sl = slice(g * POOL_GC, (g + 1) * POOL_GC)
            x = p_ref[t, :, sl]
            acc = x
            for j in range(1, w):
                src = t - j
                acc = acc + (p_ref[src, :, sl] if src >= 0 else pre_ref[POOL_BUF + src, :, sl])
            cnt = float(min(w, PAST_LEN + t + 1))
            d_list.append(acc / cnt - x)
        o_ref[t] = _pool_project(d_list, pw_ref, ps_ref)


def _pool_sample(p_tm, prefix_tm, pool_w, pool_scale):
    return pl.pallas_call(
        _pool_sample_kernel,
        out_shape=jax.ShapeDtypeStruct(p_tm.shape, F32),
        compiler_params=pltpu.CompilerParams(vmem_limit_bytes=VMEM_LIMIT),
        name="pool_sample",
    )(p_tm, prefix_tm, pool_w, pool_scale)


def _out_proj_kernel(nparts, x_ref, g_ref, w_ref, *refs):
    parts, o_ref = refs[:nparts], refs[nparts]
    y, off = None, 0
    for a_ref in parts:
        n = a_ref.shape[1]
        d = _dot(a_ref[...].astype(BF16), w_ref[off:off + n, :])
        y = d if y is None else y + d
        off += n
    o_ref[...] = x_ref[...] + _rms(y, g_ref[...])


def _out_proj(x, g, w, parts, tm):
    m = x.shape[0]
    tok = lambda i: (i, 0)
    return pl.pallas_call(
        functools.partial(_out_proj_kernel, len(parts)),
        grid=(m // tm,),
        in_specs=[pl.BlockSpec((tm, D_MODEL), tok),
                  pl.BlockSpec((1, D_MODEL), lambda i: (0, 0)),
                  pl.BlockSpec(w.shape, lambda i: (0, 0))]
        + [pl.BlockSpec((tm, a.shape[1]), tok) for a in parts],
        out_specs=pl.BlockSpec((tm, D_MODEL), tok),
        out_shape=jax.ShapeDtypeStruct((m, D_MODEL), F32),
        compiler_params=_params(("parallel",)),
        name="out_proj",
    )(x, g, w, *parts)


FFN_FC = 1408
FFN_NF = D_FF // FFN_FC


def _conv_gate(um2, um1, u, cw_ref, cb_ref):
    cw = cw_ref[...]
    return cb_ref[...] + cw[0:1] * um2 + cw[1:2] * um1 + cw[2:3] * u


def _ffn_prompt_kernel(x_ref, gpre_ref, gpost_ref, wg_ref, wv_ref, cwg_ref, cwv_ref, cbg_ref, cbv_ref,
                       wd_ref, o_ref, cg_ref, cv_ref, h_scr, acc_scr, carry_g, carry_v):
    it, f = pl.program_id(1), pl.program_id(2)
    tm = x_ref.shape[0]

    @pl.when(f == 0)
    def _():
        h_scr[...] = _rms(x_ref[...], gpre_ref[...]).astype(BF16)

    @pl.when(it == 0)
    def _():
        carry_g[f] = jnp.zeros((8, FFN_FC), F32)
        carry_v[f] = jnp.zeros((8, FFN_FC), F32)

    row = lax.broadcasted_iota(jnp.int32, (tm, FFN_FC), 0)

    def conv(w_ref, cw_ref, cb_ref, carry, tail_ref):
        u = _dot(h_scr[...], w_ref[...])
        prev = carry[f]
        p1, p2 = prev[7:8], prev[6:7]
        um1 = jnp.where(row >= 1, pltpu.roll(u, 1, 0), p1)
        um2 = jnp.where(row >= 2, pltpu.roll(u, 2, 0), jnp.where(row == 0, p2, p1))
        carry[f] = u[tm - 8:]
        tail_ref[...] = u[tm - (CONV_W - 1):]
        return _conv_gate(um2, um1, u, cw_ref, cb_ref)

    cg = conv(wg_ref, cwg_ref, cbg_ref, carry_g, cg_ref)
    cv = conv(wv_ref, cwv_ref, cbv_ref, carry_v, cv_ref)
    act = (cg * jax.nn.sigmoid(cg) * cv).astype(BF16)
    y = _dot(act, wd_ref[...])

    @pl.when(f == 0)
    def _():
        acc_scr[...] = y

    @pl.when(f > 0)
    def _():
        acc_scr[...] += y

    @pl.when(f == FFN_NF - 1)
    def _():
        o_ref[...] = x_ref[...] + _rms(acc_scr[...], gpost_ref[...])


def _ffn_prompt(x, gpre, gpost, w_up, conv_w, conv_b, w_down, tm=512):
    b, t, _ = x.shape
    fc, nf = FFN_FC, FFN_NF
    tok = lambda bi, it, f: (bi, it, 0)
    const = lambda bi, it, f: (0, 0)
    gcol = lambda bi, it, f: (0, f)
    vcol = lambda bi, it, f: (0, nf + f)
    tail = pl.BlockSpec((None, None, CONV_W - 1, fc), lambda bi, it, f: (bi, it, 0, f))
    return pl.pallas_call(
        _ffn_prompt_kernel,
        grid=(b, t // tm, nf),
        in_specs=[pl.BlockSpec((None, tm, D_MODEL), tok),
                  pl.BlockSpec((1, D_MODEL), const), pl.BlockSpec((1, D_MODEL), const),
                  pl.BlockSpec((D_MODEL, fc), gcol), pl.BlockSpec((D_MODEL, fc), vcol),
                  pl.BlockSpec((CONV_W, fc), gcol), pl.BlockSpec((CONV_W, fc), vcol),
                  pl.BlockSpec((1, fc), gcol), pl.BlockSpec((1, fc), vcol),
                  pl.BlockSpec((fc, D_MODEL), lambda bi, it, f: (f, 0))],
        out_specs=[pl.BlockSpec((None, tm, D_MODEL), tok), tail, tail],
        out_shape=[jax.ShapeDtypeStruct((b, t, D_MODEL), F32),
                   jax.ShapeDtypeStruct((b, t // tm, CONV_W - 1, D_FF), F32),
                   jax.ShapeDtypeStruct((b, t // tm, CONV_W - 1, D_FF), F32)],
        scratch_shapes=[pltpu.VMEM((tm, D_MODEL), BF16), pltpu.VMEM((tm, D_MODEL), F32),
                        pltpu.VMEM((nf, 8, fc), F32), pltpu.VMEM((nf, 8, fc), F32)],
        compiler_params=_params(("arbitrary", "arbitrary", "arbitrary")),
        name="ffn_prompt",
    )(x, gpre, gpost, w_up, w_up, conv_w, conv_w, conv_b, conv_b, w_down)


def _ffn_sample_kernel(x_ref, gpre_ref, gpost_ref, wg_ref, wv_ref, cwg_ref, cwv_ref, cbg_ref, cbv_ref,
                       wd_ref, pg_ref, pv_ref, o_ref, cg_ref, cv_ref, h_scr, acc_scr):
    f = pl.program_id(0)
    m = x_ref.shape[0]
    bd = m // DEC_SEQ

    @pl.when(f == 0)
    def _():
        h_scr[...] = _rms(x_ref[...], gpre_ref[...]).astype(BF16)

    def conv(w_ref, cw_ref, cb_ref, pre_ref, tail_ref):
        u = _dot(h_scr[...], w_ref[...])
        full = jnp.concatenate([pre_ref[...], u], axis=0)
        tail_ref[...] = u[m - (CONV_W - 1) * bd:]
        return _conv_gate(full[:m], full[bd:bd + m], u, cw_ref, cb_ref)

    cg = conv(wg_ref, cwg_ref, cbg_ref, pg_ref, cg_ref)
    cv = conv(wv_ref, cwv_ref, cbv_ref, pv_ref, cv_ref)
    act = (cg * jax.nn.sigmoid(cg) * cv).astype(BF16)
    y = _dot(act, wd_ref[...])

    @pl.when(f == 0)
    def _():
        acc_scr[...] = y

    @pl.when(f > 0)
    def _():
        acc_scr[...] += y

    @pl.when(f == FFN_NF - 1)
    def _():
        o_ref[...] = x_ref[...] + _rms(acc_scr[...], gpost_ref[...])


def _ffn_sample(x, gpre, gpost, w_up, conv_w, conv_b, w_down, prefix_tm):
    m = x.shape[0]
    bd = m // DEC_SEQ
    fc, nf = FFN_FC, FFN_NF
    const = lambda f: (0, 0)
    gcol = lambda f: (0, f)
    vcol = lambda f: (0, nf + f)
    tail = pl.BlockSpec(((CONV_W - 1) * bd, fc), gcol)
    return pl.pallas_call(
        _ffn_sample_kernel,
        grid=(nf,),
        in_specs=[pl.BlockSpec((m, D_MODEL), const),
                  pl.BlockSpec((1, D_MODEL), const), pl.BlockSpec((1, D_MODEL), const),
                  pl.BlockSpec((D_MODEL, fc), gcol), pl.BlockSpec((D_MODEL, fc), vcol),
                  pl.BlockSpec((CONV_W, fc), gcol), pl.BlockSpec((CONV_W, fc), vcol),
                  pl.BlockSpec((1, fc), gcol), pl.BlockSpec((1, fc), vcol),
                  pl.BlockSpec((fc, D_MODEL), lambda f: (f, 0)),
                  pl.BlockSpec(((CONV_W - 1) * bd, fc), gcol),
                  pl.BlockSpec(((CONV_W - 1) * bd, fc), vcol)],
        out_specs=[pl.BlockSpec((m, D_MODEL), const), tail, tail],
        out_shape=[jax.ShapeDtypeStruct((m, D_MODEL), F32),
                   jax.ShapeDtypeStruct(((CONV_W - 1) * bd, D_FF), F32),
                   jax.ShapeDtypeStruct(((CONV_W - 1) * bd, D_FF), F32)],
        scratch_shapes=[pltpu.VMEM((m, D_MODEL), BF16), pltpu.VMEM((m, D_MODEL), F32)],
        compiler_params=_params(("arbitrary",)),
        name="ffn_sample",
    )(x, gpre, gpost, w_up, w_up, conv_w, conv_w, conv_b, conv_b, w_down, prefix_tm, prefix_tm)


def _rope_tables(pos):
    half = ROT_DIM // 2
    inv = ROPE_THETA ** (-jnp.arange(0, ROT_DIM, 2, dtype=F32) / ROT_DIM)
    ang = pos.astype(F32)[:, None] * inv[None, :]
    cos, sin = jnp.cos(ang), jnp.sin(ang)
    n = pos.shape[0]
    pad = jnp.zeros((n, A_HD - ROT_DIM), F32)
    zero = jnp.zeros((n, half), F32)
    c = jnp.concatenate([cos, cos, pad + 1.0], axis=1)
    s1 = jnp.concatenate([zero, sin, pad], axis=1)
    s2 = jnp.concatenate([-sin, zero, pad], axis=1)
    return tuple(jnp.concatenate([a, a], axis=1) for a in (c, s1, s2))


def _to_bm(a_tm):
    return a_tm.reshape(DEC_SEQ, DEC_BATCH, -1).transpose(1, 0, 2)


def _to_tm(a_bm):
    return a_bm.transpose(1, 0, 2).reshape(DEC_SEQ * DEC_BATCH, -1)


def kernel(x_prompt, x_sample, cache_diff_k, cache_diff_v, state_pool, cache_fox_k, cache_fox_v, cache_fox_logf, state_ffn_conv, page_table, w_in_ab, diff_lambda, diff_subln_g, pool_w, pool_scale, w_out_ab, w_in_c, b_f, w_out_c, norm_mix_pre, norm_mix_post, norm_ffn_pre, norm_ffn_post, w_up, conv_w, conv_b, w_down):
    n_phys = cache_diff_k.shape[1]
    ck_diff = cache_diff_k.reshape(-1, n_phys, DIFF_PAGE_ROWS, LANES)
    cv_diff = cache_diff_v.reshape(-1, n_phys, DIFF_PAGE_ROWS, LANES)
    ck_fox = cache_fox_k.transpose(0, 1, 3, 4, 2).reshape(-1, n_phys, C_W, PAGE_SIZE)
    cv_fox = cache_fox_v.transpose(0, 1, 3, 4, 2).reshape(-1, n_phys, C_W, PAGE_SIZE)
    lf_fox = cache_fox_logf.transpose(0, 1, 3, 2)

    tabs_p = _rope_tables(jnp.arange(SEQ))
    tabs_s = _rope_tables(PAST_LEN + jnp.repeat(jnp.arange(DEC_SEQ), DEC_BATCH))

    xp = x_prompt.reshape(BATCH * SEQ, D_MODEL)
    xs = x_sample.transpose(1, 0, 2).reshape(DEC_SEQ * DEC_BATCH, D_MODEL)
    tm_p, tm_s = 512, DEC_SEQ * DEC_BATCH
    row = lambda a: a.reshape(1, -1)
    out = {n: [] for n in ("dk_p", "dv_p", "pl_p", "fk_p", "fv_p", "fl_p", "cv_p",
                           "dk_s", "dv_s", "pl_s", "fk_s", "fv_s", "fl_s", "cv_s")}

    for l in range(DEPTH):
        i = l // 2
        if l % 2 == 0:
            lam_init = 0.8 - 0.6 * math.exp(-0.3 * l)
            w_in = w_in_ab[i].astype(BF16)
            w_out = w_out_ab[i].astype(BF16)
            pw = pool_w[i].astype(BF16)
            small = [diff_lambda[i], row(diff_subln_g[i])]
            q, k, v, p = _inproj_even(xp, row(norm_mix_pre[l]), w_in, tabs_p, tm_p)
            b3 = lambda a: a.reshape(BATCH, SEQ, -1)
            o = _attn_prompt(False, lam_init, b3(q), b3(k), b3(v), small)
            pooled = _pool_prompt(b3(p), pw, row(pool_scale[i]))
            xp = _out_proj(xp, row(norm_mix_post[l]), w_out,
                           [o.reshape(-1, A_V_W), pooled.reshape(-1, POOL_W)], tm_p)
            out["dk_p"].append(k.reshape(BATCH, SEQ, A_HEADS, 2 * A_HD))
            out["dv_p"].append(v.reshape(BATCH, SEQ, A_HEADS, A_VD))
            out["pl_p"].append(b3(p)[:, SEQ - POOL_BUF:])
            q, k, v, p = _inproj_even(xs, row(norm_mix_pre[l]), w_in, tabs_s, tm_s)
            kb, vb = _to_bm(k), _to_bm(v)
            th = lambda a: a.reshape(DEC_BATCH, DIFF_NEW_ROWS, LANES)
            o = _decode_attn(functools.partial(_decode_diff_kernel, lam_init), "diff_attn_decode", i,
                             page_table, [th(_to_bm(q).astype(F32)), th(kb), th(vb)], small,
                             [ck_diff, cv_diff], DIFF_NEW_ROWS)
            o = o.reshape(DEC_BATCH, DEC_SEQ, A_V_W)
            pooled = _pool_sample(p.reshape(DEC_SEQ, DEC_BATCH, POOL_W),
                                  state_pool[i].transpose(1, 0, 2), pw, row(pool_scale[i]))
            xs = _out_proj(xs, row(norm_mix_post[l]), w_out,
                           [_to_tm(o), pooled.reshape(-1, POOL_W)], tm_s)
            out["dk_s"].append(kb.reshape(DEC_BATCH, DEC_SEQ, A_HEADS, 2 * A_HD))
            out["dv_s"].append(vb.reshape(DEC_BATCH, DEC_SEQ, A_HEADS, A_VD))
            out["pl_s"].append(jnp.concatenate([state_pool[i], _to_bm(p)], axis=1)[:, -POOL_BUF:])
        else:
            w_qkv = w_in_c[i][:, :3 * C_W].astype(BF16)
            w_f = w_in_c[i][:, 3 * C_W:].astype(BF16)
            w_out = w_out_c[i].astype(BF16)
            q, k, v, lf, lft = _inproj_odd(xp, row(norm_mix_pre[l]), w_qkv, w_f, w_f.T, b_f[i], tm_p)
            b3 = lambda a: a.reshape(BATCH, SEQ, -1)
            cum, cumt = _fox_cum_prompt(b3(lf), lft)
            tq = 512
            cq = cum.reshape(BATCH, SEQ, C_HEADS // 2, 2).transpose(0, 2, 1, 3)
            ck = (cumt.reshape(BATCH, C_HEADS // 2, 2, SEQ // tq, tq).transpose(0, 1, 3, 2, 4)
                  .reshape(BATCH, C_HEADS // 2, 2 * SEQ // tq, tq))
            o = _attn_prompt(True, 0.0, b3(q), b3(k), b3(v), [cq, ck], tq)
            xp = _out_proj(xp, row(norm_mix_post[l]), w_out, [o.reshape(-1, C_W)], tm_p)
            out["fk_p"].append(k.reshape(BATCH, SEQ, C_HEADS, C_HD))
            out["fv_p"].append(v.reshape(BATCH, SEQ, C_HEADS, C_HD))
            out["fl_p"].append(b3(lf))
            q, k, v, lf, lft = _inproj_odd(xs, row(norm_mix_pre[l]), w_qkv, w_f, w_f.T, b_f[i], tm_s)
            kb, vb = _to_bm(k), _to_bm(v)
            lfn = lft.reshape(C_HEADS, DEC_SEQ, DEC_BATCH).transpose(2, 0, 1)
            lfn = jnp.pad(lfn, ((0, 0), (0, 0), (0, PAGE_SIZE - DEC_SEQ)))
            o = _decode_attn(_decode_fox_kernel, "fox_attn_decode", i, page_table,
                             [_to_bm(q).astype(F32), kb, vb, lfn], [], [ck_fox, cv_fox, lf_fox], DEC_SEQ)
            xs = _out_proj(xs, row(norm_mix_post[l]), w_out, [_to_tm(o)], tm_s)
            out["fk_s"].append(kb.reshape(DEC_BATCH, DEC_SEQ, C_HEADS, C_HD))
            out["fv_s"].append(vb.reshape(DEC_BATCH, DEC_SEQ, C_HEADS, C_HD))
            out["fl_s"].append(_to_bm(lf))

        wu, wd = w_up[l].astype(BF16), w_down[l].astype(BF16)
        ffn_args = (row(norm_ffn_pre[l]), row(norm_ffn_post[l]), wu, conv_w[l], row(conv_b[l]), wd)
        xp3, tg, tv = _ffn_prompt(xp.reshape(BATCH, SEQ, D_MODEL), *ffn_args)
        xp = xp3.reshape(BATCH * SEQ, D_MODEL)
        out["cv_p"].append(jnp.concatenate([tg[:, -1], tv[:, -1]], axis=-1))
        prefix_tm = state_ffn_conv[l].transpose(1, 0, 2).reshape((CONV_W - 1) * DEC_BATCH, 2 * D_FF)
        xs, tg, tv = _ffn_sample(xs, *ffn_args, prefix_tm)
        tail = jnp.concatenate([tg, tv], axis=-1).reshape(CONV_W - 1, DEC_BATCH, 2 * D_FF)
        out["cv_s"].append(tail.transpose(1, 0, 2))

    st = {n: jnp.stack(v) for n, v in out.items()}
    y_p = xp.reshape(BATCH, SEQ, D_MODEL)
    y_s = xs.reshape(DEC_SEQ, DEC_BATCH, D_MODEL).transpose(1, 0, 2)
    return (y_p, y_s, st["dk_p"], st["dv_p"], st["pl_p"], st["fk_p"], st["fv_p"], st["fl_p"], st["cv_p"],
            st["dk_s"], st["dv_s"], st["pl_s"], st["fk_s"], st["fv_s"], st["fl_s"], st["cv_s"])
```

```python
import functools
import math

import jax
import jax.numpy as jnp
from jax import lax
from jax.experimental import pallas as pl
from jax.experimental.pallas import tpu as pltpu

D_MODEL = 1024
BATCH = 8
SEQ = 2048
DEPTH = 4
DEC_BATCH = 128
DEC_SEQ = 4
PAST_LEN = 2048
PAGE_SIZE = 128
N_PAGES = PAST_LEN // PAGE_SIZE
A_HEADS = 4
A_HD = 64
A_VD = 128
A_QK_W = 512
A_V_W = 512
POOL_WINDOWS = (2, 4, 8, 16)
POOL_W = 512
POOL_GC = 128
POOL_BUF = 15
C_HD = 64
C_HEADS = 16
C_W = 1024
ROT_DIM = 16
ROPE_THETA = 500000.0
D_FF = 2816
CONV_W = 3
EPS = 1e-6
NEG_INF = -1e30
LOG2E = math.log2(math.e)

LANES = 128
VMEM_LIMIT = 56 * 1024 * 1024

F32 = jnp.float32
BF16 = jnp.bfloat16
NT_DIMS = (((1,), (1,)), ((), ()))


def _params(sem, vmem=VMEM_LIMIT):
    return pltpu.CompilerParams(dimension_semantics=sem, vmem_limit_bytes=vmem)


def _rms(x, g):
    return x * lax.rsqrt(jnp.mean(x * x, axis=-1, keepdims=True) + EPS) * g


def _dot(a, b):
    return jnp.dot(a, b, preferred_element_type=F32)


def _dot_nt(a, b):
    return lax.dot_general(a, b, NT_DIMS, preferred_element_type=F32)


def _log_sigmoid(x):
    return jnp.minimum(x, 0.0) - jnp.log1p(jnp.exp(-jnp.abs(x)))


def _shift_down(x, s, row):
    return jnp.where(row >= s, pltpu.roll(x, s, 0), 0.0)


def _cumsum(x, axis):
    n = x.shape[axis]
    idx = lax.broadcasted_iota(jnp.int32, x.shape, axis)
    s = 1
    while s < n:
        x = x + jnp.where(idx >= s, pltpu.roll(x, s, axis), 0.0)
        s *= 2
    return x


def _rope(z, c_ref, s1_ref, s2_ref):
    outs = []
    for c in range(z.shape[1] // LANES):
        zc = z[:, c * LANES:(c + 1) * LANES]
        outs.append(zc * c_ref[...] + pltpu.roll(zc, ROT_DIM // 2, 1) * s1_ref[...]
                    + pltpu.roll(zc, LANES - ROT_DIM // 2, 1) * s2_ref[...])
    return jnp.concatenate(outs, axis=1)


def _inproj_even_kernel(x_ref, g_ref, w_ref, c_ref, s1_ref, s2_ref, *refs):
    q_ref, kb_ref, vb_ref, p_ref, k3_ref, v3_ref = refs[-6:]
    h = _rms(x_ref[...], g_ref[...]).astype(BF16)

    def block(j):
        return _dot(h, w_ref[:, j * A_QK_W:(j + 1) * A_QK_W])

    q_ref[...] = (_rope(block(0), c_ref, s1_ref, s2_ref) * (A_HD ** -0.5 * LOG2E)).astype(BF16)
    k = _rope(block(1), c_ref, s1_ref, s2_ref)
    v = block(2)
    kb_ref[...] = k.astype(BF16)
    vb_ref[...] = v.astype(BF16)
    for hd in range(A_HEADS):
        k3_ref[:, hd, :] = k[:, hd * A_VD:(hd + 1) * A_VD]
        v3_ref[:, hd, :] = v[:, hd * A_VD:(hd + 1) * A_VD]
    p_ref[...] = block(3)


def _inproj_even(x, g, w, tabs, tm, layer, stacks):
    m = x.shape[0]
    ntab = tabs[0].shape[0] // tm
    tok = lambda i: (i, 0)
    tab = pl.BlockSpec((tm, LANES), lambda i: (i % ntab, 0))
    half = pl.BlockSpec((tm, A_QK_W), tok)
    stack_spec = pl.BlockSpec((None, tm, A_HEADS, A_VD), lambda i: (layer, i, 0, 0))
    stack_shape = jax.ShapeDtypeStruct((DEPTH // 2, m, A_HEADS, A_VD), F32)
    n_in = 6
    return pl.pallas_call(
        _inproj_even_kernel,
        grid=(m // tm,),
        in_specs=[pl.BlockSpec((tm, D_MODEL), tok),
                  pl.BlockSpec((1, D_MODEL), lambda i: (0, 0)),
                  pl.BlockSpec(w.shape, lambda i: (0, 0)),
                  tab, tab, tab] + [pl.BlockSpec(memory_space=pl.ANY)] * len(stacks),
        out_specs=[half, half, half, half, stack_spec, stack_spec],
        out_shape=[jax.ShapeDtypeStruct((m, A_QK_W), BF16)] * 3
        + [jax.ShapeDtypeStruct((m, POOL_W), F32), stack_shape, stack_shape],
        input_output_aliases={n_in + s: 4 + s for s in range(len(stacks))},
        compiler_params=_params(("parallel",)),
        name="inproj_even",
    )(x, g, w, *tabs, *stacks)


def _inproj_odd_kernel(x_ref, g_ref, wq_ref, wkt_ref, wvt_ref, wf_ref, wft_ref, bf_ref, bft_ref, *refs):
    q_ref, lf_ref, lft_ref, kt_ref, vt_ref = refs[-5:]
    h = _rms(x_ref[...], g_ref[...]).astype(BF16)
    lf_ref[...] = _log_sigmoid(_dot(h, wf_ref[...]) + bf_ref[...])
    lft_ref[...] = _log_sigmoid(_dot_nt(wft_ref[...], h) + bft_ref[...])
    q_ref[...] = (_dot(h, wq_ref[...]) * (C_HD ** -0.5 * LOG2E)).astype(BF16)
    kt = _dot_nt(wkt_ref[...], h)
    vt = _dot_nt(wvt_ref[...], h)
    nseq, _, seq = kt_ref.shape
    for c in range(nseq):
        kt_ref[c] = kt[:, c * seq:(c + 1) * seq]
        vt_ref[c] = vt[:, c * seq:(c + 1) * seq]


def _inproj_odd(x, g, w_q, w_kt, w_vt, w_f, w_ft, b_f, tm, seq, layer, stacks):
    m = x.shape[0]
    tok = lambda i: (i, 0)
    const = lambda i: (0, 0)
    square = pl.BlockSpec((D_MODEL, C_W), const)
    if tm >= seq:
        stack_spec = pl.BlockSpec((None, tm // seq, C_W, seq), lambda i: (layer, i, 0, 0))
    else:
        stack_spec = pl.BlockSpec((None, 1, C_W, tm), lambda i: (layer, i // (seq // tm), 0, i % (seq // tm)))
    stack_shape = jax.ShapeDtypeStruct((DEPTH // 2, m // seq, C_W, seq), F32)
    n_in = 9
    return pl.pallas_call(
        _inproj_odd_kernel,
        grid=(m // tm,),
        in_specs=[pl.BlockSpec((tm, D_MODEL), tok),
                  pl.BlockSpec((1, D_MODEL), const),
                  square, square, square,
                  pl.BlockSpec((D_MODEL, C_HEADS), const),
                  pl.BlockSpec((C_HEADS, D_MODEL), const),
                  pl.BlockSpec((1, C_HEADS), const),
                  pl.BlockSpec((C_HEADS, 1), const)] + [pl.BlockSpec(memory_space=pl.ANY)] * len(stacks),
        out_specs=[pl.BlockSpec((tm, C_W), tok),
                   pl.BlockSpec((tm, C_HEADS), tok),
                   pl.BlockSpec((C_HEADS, tm), lambda i: (0, i)),
                   stack_spec, stack_spec],
        out_shape=[jax.ShapeDtypeStruct((m, C_W), BF16),
                   jax.ShapeDtypeStruct((m, C_HEADS), F32),
                   jax.ShapeDtypeStruct((C_HEADS, m), F32),
                   stack_shape, stack_shape],
        input_output_aliases={n_in + s: 3 + s for s in range(len(stacks))},
        compiler_params=_params(("parallel",)),
        name="inproj_odd",
    )(x, g, w_q, w_kt, w_vt, w_f, w_ft, b_f.reshape(1, C_HEADS), b_f.reshape(C_HEADS, 1), *stacks)


def _diff_lambda(dl_ref, lam_init):
    dl = dl_ref[...]
    a = jnp.sum(dl[0:1] * dl[1:2], axis=1, keepdims=True)
    b = jnp.sum(dl[2:3] * dl[3:4], axis=1, keepdims=True)
    return jnp.exp(a) - jnp.exp(b) + lam_init


ATTN_RS = 16


def _attn_prompt_kernel(fox, lam_init, tq, tk, *refs):
    iq = pl.program_id(2)
    rs = ATTN_RS
    if fox:
        (q_ref, k_ref, v_ref, cq_ref, ck_ref, o_ref,
         q2_scr, m_scr, l_scr, acc_scr, cq_scr, kb_scr, vb_scr) = refs

        @pl.when(iq == 0)
        def _():
            for c in range(kb_scr.shape[0]):
                kb_scr[c] = k_ref[:, c * tk:(c + 1) * tk].astype(BF16)
                vb_scr[c] = v_ref[:, c * tk:(c + 1) * tk].astype(BF16)
    else:
        (q_ref, k_ref, v_ref, dl_ref, g_ref, o_ref,
         q2_scr, m_scr, l_scr, acc_scr) = refs

    q = q_ref[...]
    lane = lax.broadcasted_iota(jnp.int32, q.shape, 1)
    zero = jnp.zeros_like(q)
    q2_scr[0:tq] = jnp.where(lane < A_HD, q, zero)
    q2_scr[tq:2 * tq] = jnp.where(lane >= A_HD, q, zero)
    m_scr[...] = jnp.full(m_scr.shape, NEG_INF, F32)
    l_scr[...] = jnp.zeros(l_scr.shape, F32)
    acc_scr[...] = jnp.zeros(acc_scr.shape, F32)
    if fox:
        cq = cq_ref[...]
        cq_scr[0:tq] = jnp.broadcast_to(cq[:, 0:1], (tq, LANES))
        cq_scr[tq:2 * tq] = jnp.broadcast_to(cq[:, 1:2], (tq, LANES))

    def wide(x):
        return jnp.concatenate([x] * (tk // LANES), axis=1)

    def block(ik, masked):
        start = pl.multiple_of(ik * tk, tk)
        if fox:
            s = _dot(q2_scr[...], kb_scr[ik])
        else:
            s = _dot_nt(q2_scr[...], k_ref[pl.ds(start, tk), :])
        if masked:
            col = lax.broadcasted_iota(jnp.int32, (rs, tk), 1)
            row = lax.broadcasted_iota(jnp.int32, (rs, tk), 0)
        p_parts, a_parts = [], []
        for j in range(2 * tq // rs):
            rows = slice(j * rs, (j + 1) * rs)
            sj = s[rows]
            if fox:
                ck_row = ck_ref[pl.ds(2 * ik + j * rs // tq, 1), :]
                sj = sj + wide(cq_scr[rows, :]) - ck_row
            if masked:
                sj = jnp.where(col <= row + (j * rs) % tq, sj, NEG_INF)
            m_old = m_scr[rows, :]
            m_new = jnp.maximum(m_old, jnp.max(sj, axis=1, keepdims=True))
            pj = jnp.exp2(sj - wide(m_new))
            alpha = jnp.exp2(m_old - m_new)
            l_scr[rows, :] = alpha * l_scr[rows, :] + jnp.sum(pj, axis=1, keepdims=True)
            m_scr[rows, :] = m_new
            p_parts.append(pj.astype(BF16))
            a_parts.append(alpha)
        p = jnp.concatenate(p_parts, axis=0)
        alpha = jnp.concatenate(a_parts, axis=0)
        pv = _dot_nt(p, vb_scr[ik]) if fox else _dot(p, v_ref[pl.ds(start, tk), :])
        acc_scr[...] = alpha * acc_scr[...] + pv

    def full_block(ik, carry):
        block(ik, False)
        return carry

    lax.fori_loop(0, iq, full_block, 0)
    block(iq, True)
    on = acc_scr[...] / l_scr[...]
    top, bot = on[:tq], on[tq:]
    if fox:
        o_ref[...] = jnp.where(lane < C_HD, top, bot)
    else:
        o = top - _diff_lambda(dl_ref, lam_init) * bot
        o_ref[...] = _rms(o, g_ref[...]) * (1.0 - lam_init)


def _attn_prompt(fox, lam_init, q, k, v, extra, layer=0, tq=512):
    b, t, w = q.shape
    ng, nq = w // LANES, t // tq
    tile = lambda bi, g, iq: (bi, iq, g)
    stat = pltpu.VMEM((2 * tq, LANES), F32)
    scratch = [pltpu.VMEM((2 * tq, LANES), BF16), stat, stat, stat]
    if fox:
        cq, ck = extra
        kv_spec = pl.BlockSpec((None, None, LANES, t), lambda bi, g, iq: (layer, bi, g, 0))
        extra_specs = [pl.BlockSpec((None, None, tq, 2), lambda bi, g, iq: (bi, g, iq, 0)),
                       pl.BlockSpec((None, None, 2 * nq, tq), lambda bi, g, iq: (bi, g, 0, 0))]
        scratch += [stat, pltpu.VMEM((nq, LANES, tq), BF16), pltpu.VMEM((nq, LANES, tq), BF16)]
    else:
        kv_spec = pl.BlockSpec((None, t, LANES), lambda bi, g, iq: (bi, 0, g))
        extra_specs = [pl.BlockSpec((4, A_HD), lambda bi, g, iq: (0, 0)),
                       pl.BlockSpec((1, A_VD), lambda bi, g, iq: (0, 0))]
    return pl.pallas_call(
        functools.partial(_attn_prompt_kernel, fox, lam_init, tq, tq),
        grid=(b, ng, nq),
        in_specs=[pl.BlockSpec((None, tq, LANES), tile), kv_spec, kv_spec] + extra_specs,
        out_specs=pl.BlockSpec((None, tq, LANES), tile),
        out_shape=jax.ShapeDtypeStruct((b, t, w), F32),
        scratch_shapes=scratch,
        compiler_params=_params(("parallel", "parallel", "arbitrary")),
        name="fox_attn_prompt" if fox else "diff_attn_prompt",
    )(q, k, v, *extra)


def _fox_cum_prompt_kernel(lf_ref, lft_ref, cum_ref, cumt_ref):
    cum_ref[...] = _cumsum(lf_ref[...], 0) * LOG2E
    cumt_ref[...] = _cumsum(lft_ref[...], 1) * LOG2E


def _fox_cum_prompt(lf, lft):
    b, t, h = lf.shape
    return pl.pallas_call(
        _fox_cum_prompt_kernel,
        grid=(b,),
        in_specs=[pl.BlockSpec((None, t, h), lambda i: (i, 0, 0)),
                  pl.BlockSpec((h, t), lambda i: (0, i))],
        out_specs=[pl.BlockSpec((None, t, h), lambda i: (i, 0, 0)),
                   pl.BlockSpec((None, h, t), lambda i: (i, 0, 0))],
        out_shape=[jax.ShapeDtypeStruct((b, t, h), F32),
                   jax.ShapeDtypeStruct((b, h, t), F32)],
        compiler_params=_params(("parallel",)),
        name="fox_cum_prompt",
    )(lf, lft)


KEYS_PAD = PAST_LEN + PAGE_SIZE
DIFF_PAGE_ROWS = PAGE_SIZE * A_HEADS
DIFF_NEW_ROWS = DEC_SEQ * A_HEADS


def _softmax_rows(s):
    m = jnp.max(s, axis=1, keepdims=True)
    p = jnp.exp2(s - m)
    return p.astype(BF16), jnp.sum(p, axis=1, keepdims=True)


def _decode_diff_kernel(lam_init, pt_ref, q_ref, kn_ref, vn_ref, dl_ref, g_ref, *refs):
    k_refs, v_refs, o_ref = refs[:N_PAGES], refs[N_PAGES:2 * N_PAGES], refs[2 * N_PAGES]
    q = q_ref[...]
    lane = lax.broadcasted_iota(jnp.int32, q.shape, 1)
    q2 = jnp.concatenate([jnp.where(lane < A_HD, q, 0.0),
                          jnp.where(lane >= A_HD, q, 0.0)], axis=0).astype(BF16)
    pad = jnp.zeros((DIFF_PAGE_ROWS - DIFF_NEW_ROWS, LANES), F32)
    kn = jnp.concatenate([kn_ref[...], pad], axis=0).astype(BF16)
    vn = jnp.concatenate([vn_ref[...], pad], axis=0).astype(BF16)
    parts = [_dot_nt(q2, kr[...].astype(BF16)) for kr in k_refs]
    parts.append(_dot_nt(q2, kn))
    s = jnp.concatenate(parts, axis=1)
    r = lax.broadcasted_iota(jnp.int32, s.shape, 0)
    col = lax.broadcasted_iota(jnp.int32, s.shape, 1)
    t_row = (r % DIFF_NEW_ROWS) // A_HEADS
    valid = (col % A_HEADS == r % A_HEADS) & (col // A_HEADS <= PAST_LEN + t_row)
    pb, l = _softmax_rows(jnp.where(valid, s, NEG_INF))
    o = _dot(pb[:, N_PAGES * DIFF_PAGE_ROWS:], vn)
    for j, vr in enumerate(v_refs):
        o = o + _dot(pb[:, j * DIFF_PAGE_ROWS:(j + 1) * DIFF_PAGE_ROWS], vr[...].astype(BF16))
    on = o / l
    out = on[:DIFF_NEW_ROWS] - _diff_lambda(dl_ref, lam_init) * on[DIFF_NEW_ROWS:]
    o_ref[...] = _rms(out, g_ref[...]) * (1.0 - lam_init)


def _decode_fox_kernel(pt_ref, q_ref, kn_ref, vn_ref, lfn_ref, *refs):
    kt_refs, vt_refs = refs[:N_PAGES], refs[N_PAGES:2 * N_PAGES]
    lf_refs, o_ref = refs[2 * N_PAGES:3 * N_PAGES], refs[3 * N_PAGES]
    nblk = C_HEADS
    rows = DEC_SEQ * nblk
    q4 = q_ref[...]
    qb = jnp.concatenate([jnp.broadcast_to(q4[t:t + 1], (nblk, C_W)) for t in range(DEC_SEQ)], axis=0)
    rblk = lax.broadcasted_iota(jnp.int32, (rows, C_W), 0) % nblk
    lblk = lax.broadcasted_iota(jnp.int32, (rows, C_W), 1) // C_HD
    own = rblk == lblk
    q2 = jnp.where(own, qb, 0.0).astype(BF16)
    pad = jnp.zeros((PAGE_SIZE - DEC_SEQ, C_W), F32)
    kn = jnp.concatenate([kn_ref[...], pad], axis=0).astype(BF16)
    vn = jnp.concatenate([vn_ref[...], pad], axis=0).astype(BF16)
    parts = [_dot(q2, kr[...].astype(BF16)) for kr in kt_refs]
    parts.append(_dot_nt(q2, kn))
    s = jnp.concatenate(parts, axis=1)
    ck = _cumsum(jnp.concatenate([r[...] for r in lf_refs] + [lfn_ref[...]], axis=1), 1)
    ck_rows = jnp.concatenate([ck] * DEC_SEQ, axis=0)
    cq = jnp.concatenate([ck[:, PAST_LEN + t:PAST_LEN + t + 1] for t in range(DEC_SEQ)], axis=0)
    s = s + (cq - ck_rows) * LOG2E
    t_row = lax.broadcasted_iota(jnp.int32, s.shape, 0) // nblk
    col = lax.broadcasted_iota(jnp.int32, s.shape, 1)
    pb, l = _softmax_rows(jnp.where(col <= PAST_LEN + t_row, s, NEG_INF))
    o = _dot(pb[:, PAST_LEN:], vn)
    for j, vr in enumerate(vt_refs):
        o = o + _dot_nt(pb[:, j * PAGE_SIZE:(j + 1) * PAGE_SIZE], vr[...].astype(BF16))
    on = jnp.where(own, o / l, 0.0)
    o_ref[...] = jnp.concatenate([jnp.sum(on[t * nblk:(t + 1) * nblk], axis=0, keepdims=True)
                                  for t in range(DEC_SEQ)], axis=0)


def _decode_attn(kernel, name, layer, page_table, per_batch, shared, paged, out_rows):
    bd = per_batch[0].shape[0]
    batch_spec = lambda a: pl.BlockSpec((None,) + a.shape[1:], lambda b, pt: (b, 0, 0))
    shared_spec = lambda a: pl.BlockSpec(a.shape, lambda b, pt: (0, 0))

    def page_spec(a, j):
        return pl.BlockSpec((None, None) + a.shape[2:], lambda b, pt: (layer, pt[b, j], 0, 0))

    page_specs = [page_spec(a, j) for a in paged for j in range(N_PAGES)]
    page_args = [a for a in paged for _ in range(N_PAGES)]
    out_cols = per_batch[0].shape[2]
    return pl.pallas_call(
        kernel,
        grid_spec=pltpu.PrefetchScalarGridSpec(
            num_scalar_prefetch=1,
            grid=(bd,),
            in_specs=[batch_spec(a) for a in per_batch] + [shared_spec(a) for a in shared] + page_specs,
            out_specs=pl.BlockSpec((None, out_rows, out_cols), lambda b, pt: (b, 0, 0))),
        out_shape=jax.ShapeDtypeStruct((bd, out_rows, out_cols), F32),
        compiler_params=_params(("parallel",)),
        name=name,
    )(page_table, *per_batch, *shared, *page_args)


def _pool_project(d_list, pw_ref, ps_ref):
    ps = ps_ref[...]
    outs = [_dot(d.astype(BF16), pw_ref[g]) * ps[:, g * POOL_GC:(g + 1) * POOL_GC]
            for g, d in enumerate(d_list)]
    return jnp.concatenate(outs, axis=1)


def _pool_prompt_kernel(p_ref, pw_ref, ps_ref, o_ref):
    t = p_ref.shape[0]
    row = lax.broadcasted_iota(jnp.int32, (t, POOL_GC), 0)
    d_list = []
    for g, w in enumerate(POOL_WINDOWS):
        x = p_ref[:, g * POOL_GC:(g + 1) * POOL_GC]
        acc, s = x, 1
        while s < w:
            acc = acc + _shift_down(acc, s, row)
            s *= 2
        cnt = jnp.minimum(w, row[:, 0:1] + 1).astype(F32)
        d_list.append(acc / cnt - x)
    o_ref[...] = _pool_project(d_list, pw_ref, ps_ref)


def _pool_prompt(p, pool_w, pool_scale):
    b, t, c = p.shape
    return pl.pallas_call(
        _pool_prompt_kernel,
        grid=(b,),
        in_specs=[pl.BlockSpec((None, t, c), lambda i: (i, 0, 0)),
                  pl.BlockSpec(pool_w.shape, lambda i: (0, 0, 0)),
                  pl.BlockSpec((1, c), lambda i: (0, 0))],
        out_specs=pl.BlockSpec((None, t, c), lambda i: (i, 0, 0)),
        out_shape=jax.ShapeDtypeStruct((b, t, c), F32),
        compiler_params=_params(("parallel",)),
        name="pool_prompt",
    )(p, pool_w, pool_scale)


def _pool_sample_kernel(p_ref, pre_ref, pw_ref, ps_ref, o_ref):
    for t in range(DEC_SEQ):
        d_list = []
        for g, w in enumerate(POOL_WINDOWS):
            sl = slice(g * POOL_GC, (g + 1) * POOL_GC)
            x = p_ref[t, :, sl]
            acc = x
            for j in range(1, w):
                src = t - j
                acc = acc + (p_ref[src, :, sl] if src >= 0 else pre_ref[POOL_BUF + src, :, sl])
            cnt = float(min(w, PAST_LEN + t + 1))
            d_list.append(acc / cnt - x)
        o_ref[t] = _pool_project(d_list, pw_ref, ps_ref)


def _pool_sample(p_tm, prefix_tm, pool_w, pool_scale):
    return pl.pallas_call(
        _pool_sample_kernel,
        out_shape=jax.ShapeDtypeStruct(p_tm.shape, F32),
        compiler_params=pltpu.CompilerParams(vmem_limit_bytes=VMEM_LIMIT),
        name="pool_sample",
    )(p_tm, prefix_tm, pool_w, pool_scale)


def _out_proj_kernel(nparts, x_ref, g_ref, w_ref, *refs):
    parts, o_ref = refs[:nparts], refs[nparts]
    y, off = None, 0
    for a_ref in parts:
        n = a_ref.shape[1]
        d = _dot(a_ref[...].astype(BF16), w_ref[off:off + n, :])
        y = d if y is None else y + d
        off += n
    o_ref[...] = x_ref[...] + _rms(y, g_ref[...])


def _out_proj(x, g, w, parts, tm):
    m = x.shape[0]
    tok = lambda i: (i, 0)
    return pl.pallas_call(
        functools.partial(_out_proj_kernel, len(parts)),
        grid=(m // tm,),
        in_specs=[pl.BlockSpec((tm, D_MODEL), tok),
                  pl.BlockSpec((1, D_MODEL), lambda i: (0, 0)),
                  pl.BlockSpec(w.shape, lambda i: (0, 0))]
        + [pl.BlockSpec((tm, a.shape[1]), tok) for a in parts],
        out_specs=pl.BlockSpec((tm, D_MODEL), tok),
        out_shape=jax.ShapeDtypeStruct((m, D_MODEL), F32),
        compiler_params=_params(("parallel",)),
        name="out_proj",
    )(x, g, w, *parts)


MXU_TILE = 256
FFN_CHUNKS = ((0, 6 * MXU_TILE), (6 * MXU_TILE, 5 * MXU_TILE))
RESIDENT = pl.Buffered(1)


def _ffn_chunks(h, prev_rows, wu_ref, cw_ref, cb_ref, wd_ref, tail_ref, carry_ref=None):
    m = h.shape[0]
    y = None
    for c0, width in FFN_CHUNKS:
        def conv(off):
            cols = slice(off + c0, off + c0 + width)
            u = _dot(h, wu_ref[:, cols])
            um2, um1 = prev_rows(u, cols)
            if carry_ref is not None:
                carry_ref[:, cols] = u[m - 8:]
            tail_ref[:, cols] = u[m - tail_ref.shape[0]:]
            cw = cw_ref[:, cols]
            return cb_ref[:, cols] + cw[0:1] * um2 + cw[1:2] * um1 + cw[2:3] * u

        cg = conv(0)
        cv = conv(D_FF)
        act = (cg * jax.nn.sigmoid(cg) * cv).astype(BF16)
        d = _dot(act, wd_ref[c0:c0 + width, :])
        y = d if y is None else y + d
    return y


def _ffn_prompt_kernel(x_ref, gpre_ref, gpost_ref, wu_ref, cw_ref, cb_ref, wd_ref, o_ref, tail_ref, carry):
    it = pl.program_id(1)
    tm = x_ref.shape[0]

    @pl.when(it == 0)
    def _():
        carry[...] = jnp.zeros(carry.shape, F32)

    def prev_rows(u, cols):
        row = lax.broadcasted_iota(jnp.int32, u.shape, 0)
        prev = carry[:, cols]
        p1, p2 = prev[7:8], prev[6:7]
        um1 = jnp.where(row >= 1, pltpu.roll(u, 1, 0), p1)
        um2 = jnp.where(row >= 2, pltpu.roll(u, 2, 0), jnp.where(row == 0, p2, p1))
        return um2, um1

    h = _rms(x_ref[...], gpre_ref[...]).astype(BF16)
    y = _ffn_chunks(h, prev_rows, wu_ref, cw_ref, cb_ref, wd_ref, tail_ref, carry)
    o_ref[...] = x_ref[...] + _rms(y, gpost_ref[...])


def _ffn_prompt(x, gpre, gpost, w_up, conv_w, conv_b, w_down, tm=512):
    b, t, _ = x.shape
    tok = lambda bi, it: (bi, it, 0)
    const = lambda bi, it: (0, 0)
    whole = lambda a: pl.BlockSpec(a.shape, const, pipeline_mode=RESIDENT)
    return pl.pallas_call(
        _ffn_prompt_kernel,
        grid=(b, t // tm),
        in_specs=[pl.BlockSpec((None, tm, D_MODEL), tok),
                  pl.BlockSpec((1, D_MODEL), const), pl.BlockSpec((1, D_MODEL), const),
                  whole(w_up), whole(conv_w), whole(conv_b), whole(w_down)],
        out_specs=[pl.BlockSpec((None, tm, D_MODEL), tok),
                   pl.BlockSpec((None, None, CONV_W - 1, 2 * D_FF), lambda bi, it: (bi, it, 0, 0))],
        out_shape=[jax.ShapeDtypeStruct((b, t, D_MODEL), F32),
                   jax.ShapeDtypeStruct((b, t // tm, CONV_W - 1, 2 * D_FF), F32)],
        scratch_shapes=[pltpu.VMEM((8, 2 * D_FF), F32)],
        compiler_params=_params(("arbitrary", "arbitrary")),
        name="ffn_prompt",
    )(x, gpre, gpost, w_up, conv_w, conv_b, w_down)


def _ffn_sample_kernel(x_ref, gpre_ref, gpost_ref, wu_ref, cw_ref, cb_ref, wd_ref, pre_ref, o_ref, tail_ref):
    m = x_ref.shape[0]
    bd = m // DEC_SEQ

    def prev_rows(u, cols):
        full = jnp.concatenate([pre_ref[:, cols], u], axis=0)
        return full[:m], full[bd:bd + m]

    h = _rms(x_ref[...], gpre_ref[...]).astype(BF16)
    y = _ffn_chunks(h, prev_rows, wu_ref, cw_ref, cb_ref, wd_ref, tail_ref)
    o_ref[...] = x_ref[...] + _rms(y, gpost_ref[...])


def _ffn_sample(x, gpre, gpost, w_up, conv_w, conv_b, w_down, prefix_tm):
    return pl.pallas_call(
        _ffn_sample_kernel,
        out_shape=[jax.ShapeDtypeStruct(x.shape, F32), jax.ShapeDtypeStruct(prefix_tm.shape, F32)],
        compiler_params=pltpu.CompilerParams(vmem_limit_bytes=VMEM_LIMIT),
        name="ffn_sample",
    )(x, gpre, gpost, w_up, conv_w, conv_b, w_down, prefix_tm)


def _rope_tables(pos):
    half = ROT_DIM // 2
    inv = ROPE_THETA ** (-jnp.arange(0, ROT_DIM, 2, dtype=F32) / ROT_DIM)
    ang = pos.astype(F32)[:, None] * inv[None, :]
    cos, sin = jnp.cos(ang), jnp.sin(ang)
    n = pos.shape[0]
    pad = jnp.zeros((n, A_HD - ROT_DIM), F32)
    zero = jnp.zeros((n, half), F32)
    c = jnp.concatenate([cos, cos, pad + 1.0], axis=1)
    s1 = jnp.concatenate([zero, sin, pad], axis=1)
    s2 = jnp.concatenate([-sin, zero, pad], axis=1)
    return tuple(jnp.concatenate([a, a], axis=1) for a in (c, s1, s2))


def _to_bm(a_tm):
    return a_tm.reshape(DEC_SEQ, DEC_BATCH, -1).transpose(1, 0, 2)


def _to_tm(a_bm):
    return a_bm.transpose(1, 0, 2).reshape(DEC_SEQ * DEC_BATCH, -1)


def kernel(x_prompt, x_sample, cache_diff_k, cache_diff_v, state_pool, cache_fox_k, cache_fox_v, cache_fox_logf, state_ffn_conv, page_table, w_in_ab, diff_lambda, diff_subln_g, pool_w, pool_scale, w_out_ab, w_in_c, b_f, w_out_c, norm_mix_pre, norm_mix_post, norm_ffn_pre, norm_ffn_post, w_up, conv_w, conv_b, w_down):
    n_phys = cache_diff_k.shape[1]
    ck_diff = cache_diff_k.reshape(-1, n_phys, DIFF_PAGE_ROWS, LANES)
    cv_diff = cache_diff_v.reshape(-1, n_phys, DIFF_PAGE_ROWS, LANES)
    ck_fox = cache_fox_k.transpose(0, 1, 3, 4, 2).reshape(-1, n_phys, C_W, PAGE_SIZE)
    cv_fox = cache_fox_v.transpose(0, 1, 3, 4, 2).reshape(-1, n_phys, C_W, PAGE_SIZE)
    lf_fox = cache_fox_logf.transpose(0, 1, 3, 2)

    tabs_p = _rope_tables(jnp.arange(SEQ))
    tabs_s = _rope_tables(PAST_LEN + jnp.repeat(jnp.arange(DEC_SEQ), DEC_BATCH))

    xp = x_prompt.reshape(BATCH * SEQ, D_MODEL)
    xs = x_sample.transpose(1, 0, 2).reshape(DEC_SEQ * DEC_BATCH, D_MODEL)
    tm_p, tm_s = 512, DEC_SEQ * DEC_BATCH
    row = lambda a: a.reshape(1, -1)
    out = {n: [] for n in ("pl_p", "fl_p", "cv_p", "pl_s", "fl_s", "cv_s")}
    diff_p, diff_s, fox_p, fox_s = [], [], [], []

    for l in range(DEPTH):
        i = l // 2
        if l % 2 == 0:
            lam_init = 0.8 - 0.6 * math.exp(-0.3 * l)
            w_in = w_in_ab[i].astype(BF16)
            w_out = w_out_ab[i].astype(BF16)
            pw = pool_w[i].astype(BF16)
            small = [diff_lambda[i], row(diff_subln_g[i])]
            q, k, v, p, *diff_p = _inproj_even(xp, row(norm_mix_pre[l]), w_in, tabs_p, tm_p, i, diff_p)
            b3 = lambda a: a.reshape(BATCH, SEQ, -1)
            o = _attn_prompt(False, lam_init, b3(q), b3(k), b3(v), small)
            pooled = _pool_prompt(b3(p), pw, row(pool_scale[i]))
            xp = _out_proj(xp, row(norm_mix_post[l]), w_out,
                           [o.reshape(-1, A_V_W), pooled.reshape(-1, POOL_W)], tm_p)
            out["pl_p"].append(b3(p)[:, SEQ - POOL_BUF:])
            q, _, _, p, *diff_s = _inproj_even(xs, row(norm_mix_pre[l]), w_in, tabs_s, tm_s, i, diff_s)
            th = lambda a: _to_bm(a).reshape(DEC_BATCH, DIFF_NEW_ROWS, LANES)
            o = _decode_attn(functools.partial(_decode_diff_kernel, lam_init), "diff_attn_decode", i,
                             page_table, [th(q.astype(F32)), th(diff_s[0][i]), th(diff_s[1][i])], small,
                             [ck_diff, cv_diff], DIFF_NEW_ROWS)
            o = o.reshape(DEC_BATCH, DEC_SEQ, A_V_W)
            pooled = _pool_sample(p.reshape(DEC_SEQ, DEC_BATCH, POOL_W),
                                  state_pool[i].transpose(1, 0, 2), pw, row(pool_scale[i]))
            xs = _out_proj(xs, row(norm_mix_post[l]), w_out,
                           [_to_tm(o), pooled.reshape(-1, POOL_W)], tm_s)
            out["pl_s"].append(jnp.concatenate([state_pool[i], _to_bm(p)], axis=1)[:, -POOL_BUF:])
        else:
            w_c = w_in_c[i]
            w_q = w_c[:, :C_W].astype(BF16)
            w_kt = w_c[:, C_W:2 * C_W].T.astype(BF16)
            w_vt = w_c[:, 2 * C_W:3 * C_W].T.astype(BF16)
            w_f = w_c[:, 3 * C_W:].astype(BF16)
            w_out = w_out_c[i].astype(BF16)
            odd_args = (row(norm_mix_pre[l]), w_q, w_kt, w_vt, w_f, w_f.T, b_f[i])
            q, lf, lft, *fox_p = _inproj_odd(xp, *odd_args, tm_p, SEQ, i, fox_p)
            b3 = lambda a: a.reshape(BATCH, SEQ, -1)
            cum, cumt = _fox_cum_prompt(b3(lf), lft)
            tq = 512
            cq = cum.reshape(BATCH, SEQ, C_HEADS // 2, 2).transpose(0, 2, 1, 3)
            ck = (cumt.reshape(BATCH, C_HEADS // 2, 2, SEQ // tq, tq).transpose(0, 1, 3, 2, 4)
                  .reshape(BATCH, C_HEADS // 2, 2 * SEQ // tq, tq))
            o = _attn_prompt(True, 0.0, b3(q), fox_p[0], fox_p[1], [cq, ck], i, tq)
            xp = _out_proj(xp, row(norm_mix_post[l]), w_out, [o.reshape(-1, C_W)], tm_p)
            out["fl_p"].append(b3(lf))
            q, lf, lft, *fox_s = _inproj_odd(xs, *odd_args, tm_s, DEC_BATCH, i, fox_s)
            kb, vb = (a[i].transpose(2, 0, 1) for a in fox_s)
            lfn = lft.reshape(C_HEADS, DEC_SEQ, DEC_BATCH).transpose(2, 0, 1)
            lfn = jnp.pad(lfn, ((0, 0), (0, 0), (0, PAGE_SIZE - DEC_SEQ)))
            o = _decode_attn(_decode_fox_kernel, "fox_attn_decode", i, page_table,
                             [_to_bm(q).astype(F32), kb, vb, lfn], [], [ck_fox, cv_fox, lf_fox], DEC_SEQ)
            xs = _out_proj(xs, row(norm_mix_post[l]), w_out, [_to_tm(o)], tm_s)
            out["fl_s"].append(_to_bm(lf))

        wu, wd = w_up[l].astype(BF16), w_down[l].astype(BF16)
        ffn_args = (row(norm_ffn_pre[l]), row(norm_ffn_post[l]), wu, conv_w[l], row(conv_b[l]), wd)
        xp3, tail = _ffn_prompt(xp.reshape(BATCH, SEQ, D_MODEL), *ffn_args)
        xp = xp3.reshape(BATCH * SEQ, D_MODEL)
        out["cv_p"].append(tail[:, -1])
        prefix_tm = state_ffn_conv[l].transpose(1, 0, 2).reshape((CONV_W - 1) * DEC_BATCH, 2 * D_FF)
        xs, tail = _ffn_sample(xs, *ffn_args, prefix_tm)
        out["cv_s"].append(tail.reshape(CONV_W - 1, DEC_BATCH, 2 * D_FF).transpose(1, 0, 2))

    st = {n: jnp.stack(v) for n, v in out.items()}
    n_even, n_odd = len(diff_p[0]), len(fox_p[0])
    dk_p, dv_p = (a.reshape(n_even, BATCH, SEQ, A_HEADS, A_VD) for a in diff_p)
    fk_p, fv_p = (a.reshape(n_odd, BATCH, C_HEADS, C_HD, SEQ).transpose(0, 1, 4, 2, 3) for a in fox_p)
    dk_s, dv_s = (a.reshape(n_even, DEC_SEQ, DEC_BATCH, A_HEADS, A_VD).transpose(0, 2, 1, 3, 4) for a in diff_s)
    fk_s, fv_s = (a.reshape(n_odd, DEC_SEQ, C_HEADS, C_HD, DEC_BATCH).transpose(0, 4, 1, 2, 3) for a in fox_s)
    y_p = xp.reshape(BATCH, SEQ, D_MODEL)
    y_s = xs.reshape(DEC_SEQ, DEC_BATCH, D_MODEL).transpose(1, 0, 2)
    return (y_p, y_s, dk_p, dv_p, st["pl_p"], fk_p, fv_p, st["fl_p"], st["cv_p"],
            dk_s, dv_s, st["pl_s"], fk_s, fv_s, st["fl_s"], st["cv_s"])
```

```python
import functools
import math

import jax
import jax.numpy as jnp
from jax import lax
from jax.experimental import pallas as pl
from jax.experimental.pallas import tpu as pltpu

D_MODEL = 1024
BATCH = 8
SEQ = 2048
DEPTH = 4
DEC_BATCH = 128
DEC_SEQ = 4
PAST_LEN = 2048
PAGE_SIZE = 128
N_PAGES = PAST_LEN // PAGE_SIZE
A_HEADS = 4
A_HD = 64
A_VD = 128
A_QK_W = 512
A_V_W = 512
POOL_WINDOWS = (2, 4, 8, 16)
POOL_W = 512
POOL_GC = 128
POOL_BUF = 15
C_HD = 64
C_HEADS = 16
C_W = 1024
ROT_DIM = 16
ROPE_THETA = 500000.0
D_FF = 2816
CONV_W = 3
EPS = 1e-6
NEG_INF = -1e30
LOG2E = math.log2(math.e)

LANES = 128
VMEM_LIMIT = 56 * 1024 * 1024

F32 = jnp.float32
BF16 = jnp.bfloat16
NT_DIMS = (((1,), (1,)), ((), ()))


def _params(sem, vmem=VMEM_LIMIT):
    return pltpu.CompilerParams(dimension_semantics=sem, vmem_limit_bytes=vmem)


def _rms(x, g):
    return x * lax.rsqrt(jnp.mean(x * x, axis=-1, keepdims=True) + EPS) * g


def _dot(a, b):
    return jnp.dot(a, b, preferred_element_type=F32)


def _dot_nt(a, b):
    return lax.dot_general(a, b, NT_DIMS, preferred_element_type=F32)


def _log_sigmoid(x):
    return jnp.minimum(x, 0.0) - jnp.log1p(jnp.exp(-jnp.abs(x)))


def _shift_down(x, s, row):
    return jnp.where(row >= s, pltpu.roll(x, s, 0), 0.0)


def _cumsum(x, axis):
    n = x.shape[axis]
    idx = lax.broadcasted_iota(jnp.int32, x.shape, axis)
    s = 1
    while s < n:
        x = x + jnp.where(idx >= s, pltpu.roll(x, s, axis), 0.0)
        s *= 2
    return x


def _rope(z, c_ref, s1_ref, s2_ref):
    outs = []
    for c in range(z.shape[1] // LANES):
        zc = z[:, c * LANES:(c + 1) * LANES]
        outs.append(zc * c_ref[...] + pltpu.roll(zc, ROT_DIM // 2, 1) * s1_ref[...]
                    + pltpu.roll(zc, LANES - ROT_DIM // 2, 1) * s2_ref[...])
    return jnp.concatenate(outs, axis=1)


def _inproj_even_kernel(x_ref, g_ref, w_ref, c_ref, s1_ref, s2_ref, *refs):
    q_ref, kb_ref, vb_ref, p_ref, k3_ref, v3_ref = refs[-6:]
    h = _rms(x_ref[...], g_ref[...]).astype(BF16)

    def block(j):
        return _dot(h, w_ref[:, j * A_QK_W:(j + 1) * A_QK_W])

    q_ref[...] = (_rope(block(0), c_ref, s1_ref, s2_ref) * (A_HD ** -0.5 * LOG2E)).astype(BF16)
    k = _rope(block(1), c_ref, s1_ref, s2_ref)
    v = block(2)
    kb_ref[...] = k.astype(BF16)
    vb_ref[...] = v.astype(BF16)
    for hd in range(A_HEADS):
        k3_ref[:, hd, :] = k[:, hd * A_VD:(hd + 1) * A_VD]
        v3_ref[:, hd, :] = v[:, hd * A_VD:(hd + 1) * A_VD]
    p_ref[...] = block(3)


def _inproj_even(x, g, w, tabs, tm, layer, stacks):
    m = x.shape[0]
    ntab = tabs[0].shape[0] // tm
    tok = lambda i: (i, 0)
    tab = pl.BlockSpec((tm, LANES), lambda i: (i % ntab, 0))
    half = pl.BlockSpec((tm, A_QK_W), tok)
    stack_spec = pl.BlockSpec((None, tm, A_HEADS, A_VD), lambda i: (layer, i, 0, 0))
    stack_shape = jax.ShapeDtypeStruct((DEPTH // 2, m, A_HEADS, A_VD), F32)
    n_in = 6
    return pl.pallas_call(
        _inproj_even_kernel,
        grid=(m // tm,),
        in_specs=[pl.BlockSpec((tm, D_MODEL), tok),
                  pl.BlockSpec((1, D_MODEL), lambda i: (0, 0)),
                  pl.BlockSpec(w.shape, lambda i: (0, 0)),
                  tab, tab, tab] + [pl.BlockSpec(memory_space=pl.ANY)] * len(stacks),
        out_specs=[half, half, half, half, stack_spec, stack_spec],
        out_shape=[jax.ShapeDtypeStruct((m, A_QK_W), BF16)] * 3
        + [jax.ShapeDtypeStruct((m, POOL_W), F32), stack_shape, stack_shape],
        input_output_aliases={n_in + s: 4 + s for s in range(len(stacks))},
        compiler_params=_params(("parallel",)),
        name="inproj_even",
    )(x, g, w, *tabs, *stacks)


def _inproj_odd_kernel(x_ref, g_ref, wq_ref, wkt_ref, wvt_ref, wf_ref, wft_ref, bf_ref, bft_ref, *refs):
    q_ref, lf_ref, lft_ref, kt_ref, vt_ref = refs[-5:]
    h = _rms(x_ref[...], g_ref[...]).astype(BF16)
    lf_ref[...] = _log_sigmoid(_dot(h, wf_ref[...]) + bf_ref[...])
    lft_ref[...] = _log_sigmoid(_dot_nt(wft_ref[...], h) + bft_ref[...])
    q_ref[...] = (_dot(h, wq_ref[...]) * (C_HD ** -0.5 * LOG2E)).astype(BF16)
    kt = _dot_nt(wkt_ref[...], h)
    vt = _dot_nt(wvt_ref[...], h)
    nseq, _, seq = kt_ref.shape
    for c in range(nseq):
        kt_ref[c] = kt[:, c * seq:(c + 1) * seq]
        vt_ref[c] = vt[:, c * seq:(c + 1) * seq]


def _inproj_odd(x, g, w_q, w_kt, w_vt, w_f, w_ft, b_f, tm, seq, layer, stacks):
    m = x.shape[0]
    tok = lambda i: (i, 0)
    const = lambda i: (0, 0)
    square = pl.BlockSpec((D_MODEL, C_W), const)
    if tm >= seq:
        stack_spec = pl.BlockSpec((None, tm // seq, C_W, seq), lambda i: (layer, i, 0, 0))
    else:
        stack_spec = pl.BlockSpec((None, 1, C_W, tm), lambda i: (layer, i // (seq // tm), 0, i % (seq // tm)))
    stack_shape = jax.ShapeDtypeStruct((DEPTH // 2, m // seq, C_W, seq), F32)
    n_in = 9
    return pl.pallas_call(
        _inproj_odd_kernel,
        grid=(m // tm,),
        in_specs=[pl.BlockSpec((tm, D_MODEL), tok),
                  pl.BlockSpec((1, D_MODEL), const),
                  square, square, square,
                  pl.BlockSpec((D_MODEL, C_HEADS), const),
                  pl.BlockSpec((C_HEADS, D_MODEL), const),
                  pl.BlockSpec((1, C_HEADS), const),
                  pl.BlockSpec((C_HEADS, 1), const)] + [pl.BlockSpec(memory_space=pl.ANY)] * len(stacks),
        out_specs=[pl.BlockSpec((tm, C_W), tok),
                   pl.BlockSpec((tm, C_HEADS), tok),
                   pl.BlockSpec((C_HEADS, tm), lambda i: (0, i)),
                   stack_spec, stack_spec],
        out_shape=[jax.ShapeDtypeStruct((m, C_W), BF16),
                   jax.ShapeDtypeStruct((m, C_HEADS), F32),
                   jax.ShapeDtypeStruct((C_HEADS, m), F32),
                   stack_shape, stack_shape],
        input_output_aliases={n_in + s: 3 + s for s in range(len(stacks))},
        compiler_params=_params(("parallel",)),
        name="inproj_odd",
    )(x, g, w_q, w_kt, w_vt, w_f, w_ft, b_f.reshape(1, C_HEADS), b_f.reshape(C_HEADS, 1), *stacks)


def _diff_lambda(dl_ref, lam_init):
    dl = dl_ref[...]
    a = jnp.sum(dl[0:1] * dl[1:2], axis=1, keepdims=True)
    b = jnp.sum(dl[2:3] * dl[3:4], axis=1, keepdims=True)
    return jnp.exp(a) - jnp.exp(b) + lam_init


ATTN_RS = 16


def _attn_prompt_kernel(fox, lam_init, tq, tk, *refs):
    iq = pl.program_id(2)
    rs = ATTN_RS
    if fox:
        (q_ref, k_ref, v_ref, cq_ref, ck_ref, o_ref,
         q2_scr, m_scr, l_scr, acc_scr, cq_scr, kb_scr, vb_scr) = refs

        @pl.when(iq == 0)
        def _():
            for c in range(kb_scr.shape[0]):
                kb_scr[c] = k_ref[:, c * tk:(c + 1) * tk].astype(BF16)
                vb_scr[c] = v_ref[:, c * tk:(c + 1) * tk].astype(BF16)
    else:
        (q_ref, k_ref, v_ref, dl_ref, g_ref, o_ref,
         q2_scr, m_scr, l_scr, acc_scr) = refs

    q = q_ref[...]
    lane = lax.broadcasted_iota(jnp.int32, q.shape, 1)
    zero = jnp.zeros_like(q)
    q2_scr[0:tq] = jnp.where(lane < A_HD, q, zero)
    q2_scr[tq:2 * tq] = jnp.where(lane >= A_HD, q, zero)
    m_scr[...] = jnp.full(m_scr.shape, NEG_INF, F32)
    l_scr[...] = jnp.zeros(l_scr.shape, F32)
    acc_scr[...] = jnp.zeros(acc_scr.shape, F32)
    if fox:
        cq = cq_ref[...]
        cq_scr[0:tq] = jnp.broadcast_to(cq[:, 0:1], (tq, LANES))
        cq_scr[tq:2 * tq] = jnp.broadcast_to(cq[:, 1:2], (tq, LANES))

    def wide(x):
        return jnp.concatenate([x] * (tk // LANES), axis=1)

    def block(ik, masked):
        start = pl.multiple_of(ik * tk, tk)
        if fox:
            s = _dot(q2_scr[...], kb_scr[ik])
        else:
            s = _dot_nt(q2_scr[...], k_ref[pl.ds(start, tk), :])
        if masked:
            col = lax.broadcasted_iota(jnp.int32, (rs, tk), 1)
            row = lax.broadcasted_iota(jnp.int32, (rs, tk), 0)
        p_parts, a_parts = [], []
        for j in range(2 * tq // rs):
            rows = slice(j * rs, (j + 1) * rs)
            sj = s[rows]
            if fox:
                ck_row = ck_ref[pl.ds(2 * ik + j * rs // tq, 1), :]
                sj = sj + wide(cq_scr[rows, :]) - ck_row
            if masked:
                sj = jnp.where(col <= row + (j * rs) % tq, sj, NEG_INF)
            m_old = m_scr[rows, :]
            m_new = jnp.maximum(m_old, jnp.max(sj, axis=1, keepdims=True))
            pj = jnp.exp2(sj - wide(m_new))
            alpha = jnp.exp2(m_old - m_new)
            l_scr[rows, :] = alpha * l_scr[rows, :] + jnp.sum(pj, axis=1, keepdims=True)
            m_scr[rows, :] = m_new
            p_parts.append(pj.astype(BF16))
            a_parts.append(alpha)
        p = jnp.concatenate(p_parts, axis=0)
        alpha = jnp.concatenate(a_parts, axis=0)
        pv = _dot_nt(p, vb_scr[ik]) if fox else _dot(p, v_ref[pl.ds(start, tk), :])
        acc_scr[...] = alpha * acc_scr[...] + pv

    def full_block(ik, carry):
        block(ik, False)
        return carry

    lax.fori_loop(0, iq, full_block, 0)
    block(iq, True)
    on = acc_scr[...] / l_scr[...]
    top, bot = on[:tq], on[tq:]
    if fox:
        o_ref[...] = jnp.where(lane < C_HD, top, bot)
    else:
        o = top - _diff_lambda(dl_ref, lam_init) * bot
        o_ref[...] = _rms(o, g_ref[...]) * (1.0 - lam_init)


def _attn_prompt(fox, q, k, v, extra, layer=0, tq=512):
    b, t, w = q.shape
    ng, nq = w // LANES, t // tq
    tile = lambda bi, g, iq: (bi, iq, g)
    stat = pltpu.VMEM((2 * tq, LANES), F32)
    scratch = [pltpu.VMEM((2 * tq, LANES), BF16), stat, stat, stat]
    if fox:
        cq, ck = extra
        kv_spec = pl.BlockSpec((None, None, LANES, t), lambda bi, g, iq: (layer, bi, g, 0))
        extra_specs = [pl.BlockSpec((None, None, tq, 2), lambda bi, g, iq: (bi, g, iq, 0)),
                       pl.BlockSpec((None, None, 2 * nq, tq), lambda bi, g, iq: (bi, g, 0, 0))]
        scratch += [stat, pltpu.VMEM((nq, LANES, tq), BF16), pltpu.VMEM((nq, LANES, tq), BF16)]
    else:
        kv_spec = pl.BlockSpec((None, t, LANES), lambda bi, g, iq: (bi, 0, g))
        extra_specs = [pl.BlockSpec((4, A_HD), lambda bi, g, iq: (0, 0)),
                       pl.BlockSpec((1, A_VD), lambda bi, g, iq: (0, 0))]
    return dict(grid=(b, ng, nq), tq=tq,
                in_specs=[pl.BlockSpec((None, tq, LANES), tile), kv_spec, kv_spec] + extra_specs,
                out_spec=pl.BlockSpec((None, tq, LANES), tile),
                out_shape=jax.ShapeDtypeStruct((b, t, w), F32),
                scratch=scratch, args=[q, k, v, *extra])


def _fox_cum_prompt_kernel(lf_ref, lft_ref, cum_ref, cumt_ref):
    cum_ref[...] = _cumsum(lf_ref[...], 0) * LOG2E
    cumt_ref[...] = _cumsum(lft_ref[...], 1) * LOG2E


def _fox_cum_prompt(lf, lft):
    b, t, h = lf.shape
    return pl.pallas_call(
        _fox_cum_prompt_kernel,
        grid=(b,),
        in_specs=[pl.BlockSpec((None, t, h), lambda i: (i, 0, 0)),
                  pl.BlockSpec((h, t), lambda i: (0, i))],
        out_specs=[pl.BlockSpec((None, t, h), lambda i: (i, 0, 0)),
                   pl.BlockSpec((None, h, t), lambda i: (i, 0, 0))],
        out_shape=[jax.ShapeDtypeStruct((b, t, h), F32),
                   jax.ShapeDtypeStruct((b, h, t), F32)],
        compiler_params=_params(("parallel",)),
        name="fox_cum_prompt",
    )(lf, lft)


KEYS_PAD = PAST_LEN + PAGE_SIZE
DIFF_PAGE_ROWS = PAGE_SIZE * A_HEADS
DIFF_NEW_ROWS = DEC_SEQ * A_HEADS


def _softmax_rows(s):
    m = jnp.max(s, axis=1, keepdims=True)
    p = jnp.exp2(s - m)
    return p.astype(BF16), jnp.sum(p, axis=1, keepdims=True)


def _decode_diff_kernel(lam_init, pt_ref, q_ref, kn_ref, vn_ref, dl_ref, g_ref, *refs):
    k_refs, v_refs, o_ref = refs[:N_PAGES], refs[N_PAGES:2 * N_PAGES], refs[2 * N_PAGES]
    q = q_ref[...]
    lane = lax.broadcasted_iota(jnp.int32, q.shape, 1)
    q2 = jnp.concatenate([jnp.where(lane < A_HD, q, 0.0),
                          jnp.where(lane >= A_HD, q, 0.0)], axis=0).astype(BF16)
    pad = jnp.zeros((DIFF_PAGE_ROWS - DIFF_NEW_ROWS, LANES), F32)
    kn = jnp.concatenate([kn_ref[...], pad], axis=0).astype(BF16)
    vn = jnp.concatenate([vn_ref[...], pad], axis=0).astype(BF16)
    parts = [_dot_nt(q2, kr[...].astype(BF16)) for kr in k_refs]
    parts.append(_dot_nt(q2, kn))
    s = jnp.concatenate(parts, axis=1)
    r = lax.broadcasted_iota(jnp.int32, s.shape, 0)
    col = lax.broadcasted_iota(jnp.int32, s.shape, 1)
    t_row = (r % DIFF_NEW_ROWS) // A_HEADS
    valid = (col % A_HEADS == r % A_HEADS) & (col // A_HEADS <= PAST_LEN + t_row)
    pb, l = _softmax_rows(jnp.where(valid, s, NEG_INF))
    o = _dot(pb[:, N_PAGES * DIFF_PAGE_ROWS:], vn)
    for j, vr in enumerate(v_refs):
        o = o + _dot(pb[:, j * DIFF_PAGE_ROWS:(j + 1) * DIFF_PAGE_ROWS], vr[...].astype(BF16))
    on = o / l
    out = on[:DIFF_NEW_ROWS] - _diff_lambda(dl_ref, lam_init) * on[DIFF_NEW_ROWS:]
    o_ref[...] = _rms(out, g_ref[...]) * (1.0 - lam_init)


def _decode_fox_kernel(pt_ref, q_ref, kn_ref, vn_ref, lfn_ref, *refs):
    kt_refs, vt_refs = refs[:N_PAGES], refs[N_PAGES:2 * N_PAGES]
    lf_refs, o_ref = refs[2 * N_PAGES:3 * N_PAGES], refs[3 * N_PAGES]
    nblk = C_HEADS
    rows = DEC_SEQ * nblk
    q4 = q_ref[...]
    qb = jnp.concatenate([jnp.broadcast_to(q4[t:t + 1], (nblk, C_W)) for t in range(DEC_SEQ)], axis=0)
    rblk = lax.broadcasted_iota(jnp.int32, (rows, C_W), 0) % nblk
    lblk = lax.broadcasted_iota(jnp.int32, (rows, C_W), 1) // C_HD
    own = rblk == lblk
    q2 = jnp.where(own, qb, 0.0).astype(BF16)
    pad = jnp.zeros((PAGE_SIZE - DEC_SEQ, C_W), F32)
    kn = jnp.concatenate([kn_ref[...], pad], axis=0).astype(BF16)
    vn = jnp.concatenate([vn_ref[...], pad], axis=0).astype(BF16)
    parts = [_dot(q2, kr[...].astype(BF16)) for kr in kt_refs]
    parts.append(_dot_nt(q2, kn))
    s = jnp.concatenate(parts, axis=1)
    ck = _cumsum(jnp.concatenate([r[...] for r in lf_refs] + [lfn_ref[...]], axis=1), 1)
    ck_rows = jnp.concatenate([ck] * DEC_SEQ, axis=0)
    cq = jnp.concatenate([ck[:, PAST_LEN + t:PAST_LEN + t + 1] for t in range(DEC_SEQ)], axis=0)
    s = s + (cq - ck_rows) * LOG2E
    t_row = lax.broadcasted_iota(jnp.int32, s.shape, 0) // nblk
    col = lax.broadcasted_iota(jnp.int32, s.shape, 1)
    pb, l = _softmax_rows(jnp.where(col <= PAST_LEN + t_row, s, NEG_INF))
    o = _dot(pb[:, PAST_LEN:], vn)
    for j, vr in enumerate(vt_refs):
        o = o + _dot_nt(pb[:, j * PAGE_SIZE:(j + 1) * PAGE_SIZE], vr[...].astype(BF16))
    on = jnp.where(own, o / l, 0.0)
    o_ref[...] = jnp.concatenate([jnp.sum(on[t * nblk:(t + 1) * nblk], axis=0, keepdims=True)
                                  for t in range(DEC_SEQ)], axis=0)


def _decode_attn(layer, per_batch, shared, paged, out_rows):
    bd = per_batch[0].shape[0]
    batch_spec = lambda a: pl.BlockSpec((None,) + a.shape[1:], lambda b, pt: (b, 0, 0))
    shared_spec = lambda a: pl.BlockSpec(a.shape, lambda b, pt: (0, 0))

    def page_spec(a, j):
        return pl.BlockSpec((None, None) + a.shape[2:], lambda b, pt: (layer, pt[b, j], 0, 0))

    out_cols = per_batch[0].shape[2]
    late = [False] * (len(per_batch) + len(shared)) + [n == 1 for n in range(len(paged)) for _ in range(N_PAGES)]
    return dict(late=late,
                in_specs=([batch_spec(a) for a in per_batch] + [shared_spec(a) for a in shared]
                          + [page_spec(a, j) for a in paged for j in range(N_PAGES)]),
                out_spec=pl.BlockSpec((None, out_rows, out_cols), lambda b, pt: (b, 0, 0)),
                out_shape=jax.ShapeDtypeStruct((bd, out_rows, out_cols), F32),
                args=[*per_batch, *shared, *(a for a in paged for _ in range(N_PAGES))])


def _attn_layer_kernel(prompt_kernel, decode_kernel, n_prompt_in, n_decode_in, steps_per_decode,
                       pt_ref, *refs):
    prompt_in = refs[:n_prompt_in]
    decode_in = refs[n_prompt_in:n_prompt_in + n_decode_in]
    o_prompt, o_decode = refs[n_prompt_in + n_decode_in:n_prompt_in + n_decode_in + 2]
    scratch = refs[n_prompt_in + n_decode_in + 2:]
    prompt_kernel(*prompt_in, o_prompt, *scratch)
    step = (pl.program_id(0) * pl.num_programs(1) + pl.program_id(1)) * pl.num_programs(2) + pl.program_id(2)

    @pl.when(step % steps_per_decode == 0)
    def _():
        decode_kernel(pt_ref, *decode_in, o_decode)


def _attn_layer(name, prompt_kernel, prompt, decode_kernel, decode, page_table):
    grid = prompt["grid"]
    steps = grid[0] * grid[1] * grid[2]
    bd = decode["out_shape"].shape[0]
    per = steps // bd

    def lift_prompt(spec):
        f = spec.index_map
        return pl.BlockSpec(spec.block_shape, lambda bi, g, iq, pt: f(bi, g, iq))

    def lift_decode(spec, late=False):
        f, shift = spec.index_map, (per // 2 if late else 0)
        return pl.BlockSpec(
            spec.block_shape,
            lambda bi, g, iq, pt: f(jnp.minimum((((bi * grid[1] + g) * grid[2] + iq) + shift) // per, bd - 1), pt))

    return pl.pallas_call(
        functools.partial(_attn_layer_kernel, prompt_kernel, decode_kernel,
                          len(prompt["in_specs"]), len(decode["in_specs"]), per),
        grid_spec=pltpu.PrefetchScalarGridSpec(
            num_scalar_prefetch=1,
            grid=grid,
            in_specs=[lift_prompt(s) for s in prompt["in_specs"]]
            + [lift_decode(s, late) for s, late in zip(decode["in_specs"], decode["late"])],
            out_specs=[lift_prompt(prompt["out_spec"]), lift_decode(decode["out_spec"])],
            scratch_shapes=prompt["scratch"]),
        out_shape=[prompt["out_shape"], decode["out_shape"]],
        compiler_params=_params(("arbitrary", "arbitrary", "arbitrary")),
        name=name,
    )(page_table, *prompt["args"], *decode["args"])


def _pool_project(d_list, pw_ref, ps_ref):
    ps = ps_ref[...]
    outs = [_dot(d.astype(BF16), pw_ref[g]) * ps[:, g * POOL_GC:(g + 1) * POOL_GC]
            for g, d in enumerate(d_list)]
    return jnp.concatenate(outs, axis=1)


def _pool_prompt_kernel(p_ref, pw_ref, ps_ref, o_ref):
    t = p_ref.shape[0]
    row = lax.broadcasted_iota(jnp.int32, (t, POOL_GC), 0)
    d_list = []
    for g, w in enumerate(POOL_WINDOWS):
        x = p_ref[:, g * POOL_GC:(g + 1) * POOL_GC]
        acc, s = x, 1
        while s < w:
            acc = acc + _shift_down(acc, s, row)
            s *= 2
        cnt = jnp.minimum(w, row[:, 0:1] + 1).astype(F32)
        d_list.append(acc / cnt - x)
    o_ref[...] = _pool_project(d_list, pw_ref, ps_ref)


def _pool_prompt(p, pool_w, pool_scale):
    b, t, c = p.shape
    return pl.pallas_call(
        _pool_prompt_kernel,
        grid=(b,),
        in_specs=[pl.BlockSpec((None, t, c), lambda i: (i, 0, 0)),
                  pl.BlockSpec(pool_w.shape, lambda i: (0, 0, 0)),
                  pl.BlockSpec((1, c), lambda i: (0, 0))],
        out_specs=pl.BlockSpec((None, t, c), lambda i: (i, 0, 0)),
        out_shape=jax.ShapeDtypeStruct((b, t, c), F32),
        compiler_params=_params(("parallel",)),
        name="pool_prompt",
    )(p, pool_w, pool_scale)


def _pool_sample_kernel(p_ref, pre_ref, pw_ref, ps_ref, o_ref):
    for t in range(DEC_SEQ):
        d_list = []
        for g, w in enumerate(POOL_WINDOWS):
            sl = slice(g * POOL_GC, (g + 1) * POOL_GC)
            x = p_ref[t, :, sl]
            acc = x
            for j in range(1, w):
                src = t - j
                acc = acc + (p_ref[src, :, sl] if src >= 0 else pre_ref[POOL_BUF + src, :, sl])
            cnt = float(min(w, PAST_LEN + t + 1))
            d_list.append(acc / cnt - x)
        o_ref[t] = _pool_project(d_list, pw_ref, ps_ref)


def _pool_sample(p_tm, prefix_tm, pool_w, pool_scale):
    return pl.pallas_call(
        _pool_sample_kernel,
        out_shape=jax.ShapeDtypeStruct(p_tm.shape, F32),
        compiler_params=pltpu.CompilerParams(vmem_limit_bytes=VMEM_LIMIT),
        name="pool_sample",
    )(p_tm, prefix_tm, pool_w, pool_scale)


def _out_proj_kernel(nparts, x_ref, g_ref, w_ref, *refs):
    parts, o_ref = refs[:nparts], refs[nparts]
    y, off = None, 0
    for a_ref in parts:
        n = a_ref.shape[1]
        d = _dot(a_ref[...].astype(BF16), w_ref[off:off + n, :])
        y = d if y is None else y + d
        off += n
    o_ref[...] = x_ref[...] + _rms(y, g_ref[...])


def _out_proj(x, g, w, parts, tm):
    m = x.shape[0]
    tok = lambda i: (i, 0)
    return pl.pallas_call(
        functools.partial(_out_proj_kernel, len(parts)),
        grid=(m // tm,),
        in_specs=[pl.BlockSpec((tm, D_MODEL), tok),
                  pl.BlockSpec((1, D_MODEL), lambda i: (0, 0)),
                  pl.BlockSpec(w.shape, lambda i: (0, 0))]
        + [pl.BlockSpec((tm, a.shape[1]), tok) for a in parts],
        out_specs=pl.BlockSpec((tm, D_MODEL), tok),
        out_shape=jax.ShapeDtypeStruct((m, D_MODEL), F32),
        compiler_params=_params(("parallel",)),
        name="out_proj",
    )(x, g, w, *parts)


MXU_TILE = 256
FFN_CHUNKS = ((0, 6 * MXU_TILE), (6 * MXU_TILE, 5 * MXU_TILE))
RESIDENT = pl.Buffered(1)


def _ffn_chunks(h, prev_rows, wu_ref, cw_ref, cb_ref, wd_ref, tail_ref, carry_ref=None):
    m = h.shape[0]
    y = None
    for c0, width in FFN_CHUNKS:
        def conv(off):
            cols = slice(off + c0, off + c0 + width)
            u = _dot(h, wu_ref[:, cols])
            um2, um1 = prev_rows(u, cols)
            if carry_ref is not None:
                carry_ref[:, cols] = u[m - 8:]
            tail_ref[:, cols] = u[m - tail_ref.shape[0]:]
            cw = cw_ref[:, cols]
            return cb_ref[:, cols] + cw[0:1] * um2 + cw[1:2] * um1 + cw[2:3] * u

        cg = conv(0)
        cv = conv(D_FF)
        act = (cg * jax.nn.sigmoid(cg) * cv).astype(BF16)
        d = _dot(act, wd_ref[c0:c0 + width, :])
        y = d if y is None else y + d
    return y


def _ffn_prompt_kernel(x_ref, gpre_ref, gpost_ref, wu_ref, cw_ref, cb_ref, wd_ref, o_ref, tail_ref, carry):
    it = pl.program_id(1)
    tm = x_ref.shape[0]

    @pl.when(it == 0)
    def _():
        carry[...] = jnp.zeros(carry.shape, F32)

    def prev_rows(u, cols):
        row = lax.broadcasted_iota(jnp.int32, u.shape, 0)
        prev = carry[:, cols]
        p1, p2 = prev[7:8], prev[6:7]
        um1 = jnp.where(row >= 1, pltpu.roll(u, 1, 0), p1)
        um2 = jnp.where(row >= 2, pltpu.roll(u, 2, 0), jnp.where(row == 0, p2, p1))
        return um2, um1

    h = _rms(x_ref[...], gpre_ref[...]).astype(BF16)
    y = _ffn_chunks(h, prev_rows, wu_ref, cw_ref, cb_ref, wd_ref, tail_ref, carry)
    o_ref[...] = x_ref[...] + _rms(y, gpost_ref[...])


def _ffn_prompt(x, gpre, gpost, w_up, conv_w, conv_b, w_down, tm=512):
    b, t, _ = x.shape
    tok = lambda bi, it: (bi, it, 0)
    const = lambda bi, it: (0, 0)
    whole = lambda a: pl.BlockSpec(a.shape, const, pipeline_mode=RESIDENT)
    return pl.pallas_call(
        _ffn_prompt_kernel,
        grid=(b, t // tm),
        in_specs=[pl.BlockSpec((None, tm, D_MODEL), tok),
                  pl.BlockSpec((1, D_MODEL), const), pl.BlockSpec((1, D_MODEL), const),
                  whole(w_up), whole(conv_w), whole(conv_b), whole(w_down)],
        out_specs=[pl.BlockSpec((None, tm, D_MODEL), tok),
                   pl.BlockSpec((None, None, CONV_W - 1, 2 * D_FF), lambda bi, it: (bi, it, 0, 0))],
        out_shape=[jax.ShapeDtypeStruct((b, t, D_MODEL), F32),
                   jax.ShapeDtypeStruct((b, t // tm, CONV_W - 1, 2 * D_FF), F32)],
        scratch_shapes=[pltpu.VMEM((8, 2 * D_FF), F32)],
        compiler_params=_params(("arbitrary", "arbitrary")),
        name="ffn_prompt",
    )(x, gpre, gpost, w_up, conv_w, conv_b, w_down)


def _ffn_sample_kernel(x_ref, gpre_ref, gpost_ref, wu_ref, cw_ref, cb_ref, wd_ref, pre_ref, o_ref, tail_ref):
    m = x_ref.shape[0]
    bd = m // DEC_SEQ

    def prev_rows(u, cols):
        full = jnp.concatenate([pre_ref[:, cols], u], axis=0)
        return full[:m], full[bd:bd + m]

    h = _rms(x_ref[...], gpre_ref[...]).astype(BF16)
    y = _ffn_chunks(h, prev_rows, wu_ref, cw_ref, cb_ref, wd_ref, tail_ref)
    o_ref[...] = x_ref[...] + _rms(y, gpost_ref[...])


def _ffn_sample(x, gpre, gpost, w_up, conv_w, conv_b, w_down, prefix_tm):
    return pl.pallas_call(
        _ffn_sample_kernel,
        out_shape=[jax.ShapeDtypeStruct(x.shape, F32), jax.ShapeDtypeStruct(prefix_tm.shape, F32)],
        compiler_params=pltpu.CompilerParams(vmem_limit_bytes=VMEM_LIMIT),
        name="ffn_sample",
    )(x, gpre, gpost, w_up, conv_w, conv_b, w_down, prefix_tm)


def _rope_tables(pos):
    half = ROT_DIM // 2
    inv = ROPE_THETA ** (-jnp.arange(0, ROT_DIM, 2, dtype=F32) / ROT_DIM)
    ang = pos.astype(F32)[:, None] * inv[None, :]
    cos, sin = jnp.cos(ang), jnp.sin(ang)
    n = pos.shape[0]
    pad = jnp.zeros((n, A_HD - ROT_DIM), F32)
    zero = jnp.zeros((n, half), F32)
    c = jnp.concatenate([cos, cos, pad + 1.0], axis=1)
    s1 = jnp.concatenate([zero, sin, pad], axis=1)
    s2 = jnp.concatenate([-sin, zero, pad], axis=1)
    return tuple(jnp.concatenate([a, a], axis=1) for a in (c, s1, s2))


def _to_bm(a_tm):
    return a_tm.reshape(DEC_SEQ, DEC_BATCH, -1).transpose(1, 0, 2)


def _to_tm(a_bm):
    return a_bm.transpose(1, 0, 2).reshape(DEC_SEQ * DEC_BATCH, -1)


def kernel(x_prompt, x_sample, cache_diff_k, cache_diff_v, state_pool, cache_fox_k, cache_fox_v, cache_fox_logf, state_ffn_conv, page_table, w_in_ab, diff_lambda, diff_subln_g, pool_w, pool_scale, w_out_ab, w_in_c, b_f, w_out_c, norm_mix_pre, norm_mix_post, norm_ffn_pre, norm_ffn_post, w_up, conv_w, conv_b, w_down):
    n_phys = cache_diff_k.shape[1]
    ck_diff = cache_diff_k.reshape(-1, n_phys, DIFF_PAGE_ROWS, LANES)
    cv_diff = cache_diff_v.reshape(-1, n_phys, DIFF_PAGE_ROWS, LANES)
    ck_fox = cache_fox_k.transpose(0, 1, 3, 4, 2).reshape(-1, n_phys, C_W, PAGE_SIZE)
    cv_fox = cache_fox_v.transpose(0, 1, 3, 4, 2).reshape(-1, n_phys, C_W, PAGE_SIZE)
    lf_fox = cache_fox_logf.transpose(0, 1, 3, 2)

    tabs_p = _rope_tables(jnp.arange(SEQ))
    tabs_s = _rope_tables(PAST_LEN + jnp.repeat(jnp.arange(DEC_SEQ), DEC_BATCH))

    xp = x_prompt.reshape(BATCH * SEQ, D_MODEL)
    xs = x_sample.transpose(1, 0, 2).reshape(DEC_SEQ * DEC_BATCH, D_MODEL)
    tm_p, tm_s = 512, DEC_SEQ * DEC_BATCH
    row = lambda a: a.reshape(1, -1)
    out = {n: [] for n in ("pl_p", "fl_p", "cv_p", "pl_s", "fl_s", "cv_s")}
    diff_p, diff_s, fox_p, fox_s = [], [], [], []

    for l in range(DEPTH):
        i = l // 2
        if l % 2 == 0:
            lam_init = 0.8 - 0.6 * math.exp(-0.3 * l)
            w_in = w_in_ab[i].astype(BF16)
            w_out = w_out_ab[i].astype(BF16)
            pw = pool_w[i].astype(BF16)
            small = [diff_lambda[i], row(diff_subln_g[i])]
            q, k, v, p, *diff_p = _inproj_even(xp, row(norm_mix_pre[l]), w_in, tabs_p, tm_p, i, diff_p)
            qs, _, _, ps, *diff_s = _inproj_even(xs, row(norm_mix_pre[l]), w_in, tabs_s, tm_s, i, diff_s)
            b3 = lambda a: a.reshape(BATCH, SEQ, -1)
            th = lambda a: _to_bm(a).reshape(DEC_BATCH, DIFF_NEW_ROWS, LANES)
            tq = 512
            o, os_ = _attn_layer(
                "diff_attn",
                functools.partial(_attn_prompt_kernel, False, lam_init, tq, tq),
                _attn_prompt(False, b3(q), b3(k), b3(v), small, tq=tq),
                functools.partial(_decode_diff_kernel, lam_init),
                _decode_attn(i, [th(qs.astype(F32)), th(diff_s[0][i]), th(diff_s[1][i])], small,
                             [ck_diff, cv_diff], DIFF_NEW_ROWS),
                page_table)
            pooled = _pool_prompt(b3(p), pw, row(pool_scale[i]))
            xp = _out_proj(xp, row(norm_mix_post[l]), w_out,
                           [o.reshape(-1, A_V_W), pooled.reshape(-1, POOL_W)], tm_p)
            out["pl_p"].append(b3(p)[:, SEQ - POOL_BUF:])
            os_ = os_.reshape(DEC_BATCH, DEC_SEQ, A_V_W)
            pooled = _pool_sample(ps.reshape(DEC_SEQ, DEC_BATCH, POOL_W),
                                  state_pool[i].transpose(1, 0, 2), pw, row(pool_scale[i]))
            xs = _out_proj(xs, row(norm_mix_post[l]), w_out,
                           [_to_tm(os_), pooled.reshape(-1, POOL_W)], tm_s)
            out["pl_s"].append(jnp.concatenate([state_pool[i], _to_bm(ps)], axis=1)[:, -POOL_BUF:])
        else:
            w_c = w_in_c[i]
            w_q = w_c[:, :C_W].astype(BF16)
            w_kt = w_c[:, C_W:2 * C_W].T.astype(BF16)
            w_vt = w_c[:, 2 * C_W:3 * C_W].T.astype(BF16)
            w_f = w_c[:, 3 * C_W:].astype(BF16)
            w_out = w_out_c[i].astype(BF16)
            odd_args = (row(norm_mix_pre[l]), w_q, w_kt, w_vt, w_f, w_f.T, b_f[i])
            q, lf, lft, *fox_p = _inproj_odd(xp, *odd_args, tm_p, SEQ, i, fox_p)
            qs, lfs, lfts, *fox_s = _inproj_odd(xs, *odd_args, tm_s, DEC_BATCH, i, fox_s)
            b3 = lambda a: a.reshape(BATCH, SEQ, -1)
            cum, cumt = _fox_cum_prompt(b3(lf), lft)
            tq = 512
            cq = cum.reshape(BATCH, SEQ, C_HEADS // 2, 2).transpose(0, 2, 1, 3)
            ck = (cumt.reshape(BATCH, C_HEADS // 2, 2, SEQ // tq, tq).transpose(0, 1, 3, 2, 4)
                  .reshape(BATCH, C_HEADS // 2, 2 * SEQ // tq, tq))
            kb, vb = (a[i].transpose(2, 0, 1) for a in fox_s)
            lfn = lfts.reshape(C_HEADS, DEC_SEQ, DEC_BATCH).transpose(2, 0, 1)
            lfn = jnp.pad(lfn, ((0, 0), (0, 0), (0, PAGE_SIZE - DEC_SEQ)))
            o, os_ = _attn_layer(
                "fox_attn",
                functools.partial(_attn_prompt_kernel, True, 0.0, tq, tq),
                _attn_prompt(True, b3(q), fox_p[0], fox_p[1], [cq, ck], i, tq),
                _decode_fox_kernel,
                _decode_attn(i, [_to_bm(qs).astype(F32), kb, vb, lfn], [], [ck_fox, cv_fox, lf_fox], DEC_SEQ),
                page_table)
            xp = _out_proj(xp, row(norm_mix_post[l]), w_out, [o.reshape(-1, C_W)], tm_p)
            out["fl_p"].append(b3(lf))
            xs = _out_proj(xs, row(norm_mix_post[l]), w_out, [_to_tm(os_)], tm_s)
            out["fl_s"].append(_to_bm(lfs))

        wu, wd = w_up[l].astype(BF16), w_down[l].astype(BF16)
        ffn_args = (row(norm_ffn_pre[l]), row(norm_ffn_post[l]), wu, conv_w[l], row(conv_b[l]), wd)
        xp3, tail = _ffn_prompt(xp.reshape(BATCH, SEQ, D_MODEL), *ffn_args)
        xp = xp3.reshape(BATCH * SEQ, D_MODEL)
        out["cv_p"].append(tail[:, -1])
        prefix_tm = state_ffn_conv[l].transpose(1, 0, 2).reshape((CONV_W - 1) * DEC_BATCH, 2 * D_FF)
        xs, tail = _ffn_sample(xs, *ffn_args, prefix_tm)
        out["cv_s"].append(tail.reshape(CONV_W - 1, DEC_BATCH, 2 * D_FF).transpose(1, 0, 2))

    st = {n: jnp.stack(v) for n, v in out.items()}
    n_even, n_odd = len(diff_p[0]), len(fox_p[0])
    dk_p, dv_p = (a.reshape(n_even, BATCH, SEQ, A_HEADS, A_VD) for a in diff_p)
    fk_p, fv_p = (a.reshape(n_odd, BATCH, C_HEADS, C_HD, SEQ).transpose(0, 1, 4, 2, 3) for a in fox_p)
    dk_s, dv_s = (a.reshape(n_even, DEC_SEQ, DEC_BATCH, A_HEADS, A_VD).transpose(0, 2, 1, 3, 4) for a in diff_s)
    fk_s, fv_s = (a.reshape(n_odd, DEC_SEQ, C_HEADS, C_HD, DEC_BATCH).transpose(0, 4, 1, 2, 3) for a in fox_s)
    y_p = xp.reshape(BATCH, SEQ, D_MODEL)
    y_s = xs.reshape(DEC_SEQ, DEC_BATCH, D_MODEL).transpose(1, 0, 2)
    return (y_p, y_s, dk_p, dv_p, st["pl_p"], fk_p, fv_p, st["fl_p"], st["cv_p"],
            dk_s, dv_s, st["pl_s"], fk_s, fv_s, st["fl_s"], st["cv_s"])
```

```python
import functools
import math

import jax
import jax.numpy as jnp
from jax import lax
from jax.experimental import pallas as pl
from jax.experimental.pallas import tpu as pltpu

D_MODEL = 1024
BATCH = 8
SEQ = 2048
DEPTH = 4
DEC_BATCH = 128
DEC_SEQ = 4
PAST_LEN = 2048
PAGE_SIZE = 128
N_PAGES = PAST_LEN // PAGE_SIZE
A_HEADS = 4
A_HD = 64
A_VD = 128
A_QK_W = 512
A_V_W = 512
POOL_WINDOWS = (2, 4, 8, 16)
POOL_W = 512
POOL_GC = 128
POOL_BUF = 15
C_HD = 64
C_HEADS = 16
C_W = 1024
ROT_DIM = 16
ROPE_THETA = 500000.0
D_FF = 2816
CONV_W = 3
EPS = 1e-6
NEG_INF = -1e30
LOG2E = math.log2(math.e)

LANES = 128
VMEM_LIMIT = 56 * 1024 * 1024

F32 = jnp.float32
BF16 = jnp.bfloat16
NT_DIMS = (((1,), (1,)), ((), ()))


def _params(sem, vmem=VMEM_LIMIT):
    return pltpu.CompilerParams(dimension_semantics=sem, vmem_limit_bytes=vmem)


def _rms(x, g):
    return x * lax.rsqrt(jnp.mean(x * x, axis=-1, keepdims=True) + EPS) * g


def _dot(a, b):
    return jnp.dot(a, b, preferred_element_type=F32)


def _dot_nt(a, b):
    return lax.dot_general(a, b, NT_DIMS, preferred_element_type=F32)


def _log_sigmoid(x):
    return jnp.minimum(x, 0.0) - jnp.log1p(jnp.exp(-jnp.abs(x)))


def _shift_down(x, s, row):
    return jnp.where(row >= s, pltpu.roll(x, s, 0), 0.0)


def _cumsum(x, axis):
    n = x.shape[axis]
    idx = lax.broadcasted_iota(jnp.int32, x.shape, axis)
    s = 1
    while s < n:
        x = x + jnp.where(idx >= s, pltpu.roll(x, s, axis), 0.0)
        s *= 2
    return x


def _rope(z, c_ref, s1_ref, s2_ref):
    outs = []
    for c in range(z.shape[1] // LANES):
        zc = z[:, c * LANES:(c + 1) * LANES]
        outs.append(zc * c_ref[...] + pltpu.roll(zc, ROT_DIM // 2, 1) * s1_ref[...]
                    + pltpu.roll(zc, LANES - ROT_DIM // 2, 1) * s2_ref[...])
    return jnp.concatenate(outs, axis=1)


def _inproj_even_kernel(x_ref, g_ref, w_ref, c_ref, s1_ref, s2_ref, *refs):
    q_ref, kb_ref, vb_ref, p_ref, k3_ref, v3_ref = refs[-6:]
    h = _rms(x_ref[...], g_ref[...]).astype(BF16)

    def block(j):
        return _dot(h, w_ref[:, j * A_QK_W:(j + 1) * A_QK_W])

    q_ref[...] = (_rope(block(0), c_ref, s1_ref, s2_ref) * (A_HD ** -0.5 * LOG2E)).astype(BF16)
    k = _rope(block(1), c_ref, s1_ref, s2_ref)
    v = block(2)
    kb_ref[...] = k.astype(BF16)
    vb_ref[...] = v.astype(BF16)
    for hd in range(A_HEADS):
        k3_ref[:, hd, :] = k[:, hd * A_VD:(hd + 1) * A_VD]
        v3_ref[:, hd, :] = v[:, hd * A_VD:(hd + 1) * A_VD]
    p_ref[...] = block(3)


def _inproj_even(x, g, w, tabs, tm, layer, stacks):
    m = x.shape[0]
    ntab = tabs[0].shape[0] // tm
    tok = lambda i: (i, 0)
    tab = pl.BlockSpec((tm, LANES), lambda i: (i % ntab, 0))
    half = pl.BlockSpec((tm, A_QK_W), tok)
    stack_spec = pl.BlockSpec((None, tm, A_HEADS, A_VD), lambda i: (layer, i, 0, 0))
    stack_shape = jax.ShapeDtypeStruct((DEPTH // 2, m, A_HEADS, A_VD), F32)
    n_in = 6
    return pl.pallas_call(
        _inproj_even_kernel,
        grid=(m // tm,),
        in_specs=[pl.BlockSpec((tm, D_MODEL), tok),
                  pl.BlockSpec((1, D_MODEL), lambda i: (0, 0)),
                  pl.BlockSpec(w.shape, lambda i: (0, 0)),
                  tab, tab, tab] + [pl.BlockSpec(memory_space=pl.ANY)] * len(stacks),
        out_specs=[half, half, half, half, stack_spec, stack_spec],
        out_shape=[jax.ShapeDtypeStruct((m, A_QK_W), BF16)] * 3
        + [jax.ShapeDtypeStruct((m, POOL_W), F32), stack_shape, stack_shape],
        input_output_aliases={n_in + s: 4 + s for s in range(len(stacks))},
        compiler_params=_params(("parallel",)),
        name="inproj_even",
    )(x, g, w, *tabs, *stacks)


def _inproj_odd_kernel(x_ref, g_ref, wq_ref, wkt_ref, wvt_ref, wf_ref, wft_ref, bf_ref, bft_ref, *refs):
    q_ref, lf_ref, lft_ref, kt_ref, vt_ref = refs[-5:]
    h = _rms(x_ref[...], g_ref[...]).astype(BF16)
    lf_ref[...] = _log_sigmoid(_dot(h, wf_ref[...]) + bf_ref[...])
    lft_ref[...] = _log_sigmoid(_dot_nt(wft_ref[...], h) + bft_ref[...])
    q_ref[...] = (_dot(h, wq_ref[...]) * (C_HD ** -0.5 * LOG2E)).astype(BF16)
    kt = _dot_nt(wkt_ref[...], h)
    vt = _dot_nt(wvt_ref[...], h)
    nseq, _, seq = kt_ref.shape
    for c in range(nseq):
        kt_ref[c] = kt[:, c * seq:(c + 1) * seq]
        vt_ref[c] = vt[:, c * seq:(c + 1) * seq]


def _inproj_odd(x, g, w_q, w_kt, w_vt, w_f, w_ft, b_f, tm, seq, layer, stacks):
    m = x.shape[0]
    tok = lambda i: (i, 0)
    const = lambda i: (0, 0)
    square = pl.BlockSpec((D_MODEL, C_W), const)
    if tm >= seq:
        stack_spec = pl.BlockSpec((None, tm // seq, C_W, seq), lambda i: (layer, i, 0, 0))
    else:
        stack_spec = pl.BlockSpec((None, 1, C_W, tm), lambda i: (layer, i // (seq // tm), 0, i % (seq // tm)))
    stack_shape = jax.ShapeDtypeStruct((DEPTH // 2, m // seq, C_W, seq), F32)
    n_in = 9
    return pl.pallas_call(
        _inproj_odd_kernel,
        grid=(m // tm,),
        in_specs=[pl.BlockSpec((tm, D_MODEL), tok),
                  pl.BlockSpec((1, D_MODEL), const),
                  square, square, square,
                  pl.BlockSpec((D_MODEL, C_HEADS), const),
                  pl.BlockSpec((C_HEADS, D_MODEL), const),
                  pl.BlockSpec((1, C_HEADS), const),
                  pl.BlockSpec((C_HEADS, 1), const)] + [pl.BlockSpec(memory_space=pl.ANY)] * len(stacks),
        out_specs=[pl.BlockSpec((tm, C_W), tok),
                   pl.BlockSpec((tm, C_HEADS), tok),
                   pl.BlockSpec((C_HEADS, tm), lambda i: (0, i)),
                   stack_spec, stack_spec],
        out_shape=[jax.ShapeDtypeStruct((m, C_W), BF16),
                   jax.ShapeDtypeStruct((m, C_HEADS), F32),
                   jax.ShapeDtypeStruct((C_HEADS, m), F32),
                   stack_shape, stack_shape],
        input_output_aliases={n_in + s: 3 + s for s in range(len(stacks))},
        compiler_params=_params(("parallel",)),
        name="inproj_odd",
    )(x, g, w_q, w_kt, w_vt, w_f, w_ft, b_f.reshape(1, C_HEADS), b_f.reshape(C_HEADS, 1), *stacks)


def _diff_lambda(dl_ref, lam_init):
    dl = dl_ref[...]
    a = jnp.sum(dl[0:1] * dl[1:2], axis=1, keepdims=True)
    b = jnp.sum(dl[2:3] * dl[3:4], axis=1, keepdims=True)
    return jnp.exp(a) - jnp.exp(b) + lam_init


ATTN_RS = 16


def _attn_prompt_kernel(fox, lam_init, tq, tk, *refs):
    iq = pl.program_id(2)
    rs = ATTN_RS
    if fox:
        (q_ref, k_ref, v_ref, cq_ref, ck_ref, o_ref,
         q2_scr, m_scr, l_scr, acc_scr, cq_scr, kb_scr, vb_scr) = refs

        @pl.when(iq == 0)
        def _():
            for c in range(kb_scr.shape[0]):
                kb_scr[c] = k_ref[:, c * tk:(c + 1) * tk].astype(BF16)
                vb_scr[c] = v_ref[:, c * tk:(c + 1) * tk].astype(BF16)
    else:
        (q_ref, k_ref, v_ref, dl_ref, g_ref, o_ref,
         q2_scr, m_scr, l_scr, acc_scr) = refs

    q = q_ref[...]
    lane = lax.broadcasted_iota(jnp.int32, q.shape, 1)
    zero = jnp.zeros_like(q)
    q2_scr[0:tq] = jnp.where(lane < A_HD, q, zero)
    q2_scr[tq:2 * tq] = jnp.where(lane >= A_HD, q, zero)
    m_scr[...] = jnp.full(m_scr.shape, NEG_INF, F32)
    l_scr[...] = jnp.zeros(l_scr.shape, F32)
    acc_scr[...] = jnp.zeros(acc_scr.shape, F32)
    if fox:
        cq = cq_ref[...]
        head = lax.broadcasted_iota(jnp.int32, cq.shape, 1) - 2 * pl.program_id(1)
        for s in range(2):
            col = jnp.sum(jnp.where(head == s, cq, 0.0), axis=1, keepdims=True)
            cq_scr[s * tq:(s + 1) * tq] = jnp.broadcast_to(col, (tq, LANES))

    def wide(x):
        return jnp.concatenate([x] * (tk // LANES), axis=1)

    def block(ik, masked):
        start = pl.multiple_of(ik * tk, tk)
        if fox:
            s = _dot(q2_scr[...], kb_scr[ik])
        else:
            s = _dot_nt(q2_scr[...], k_ref[pl.ds(start, tk), :])
        if masked:
            col = lax.broadcasted_iota(jnp.int32, (rs, tk), 1)
            row = lax.broadcasted_iota(jnp.int32, (rs, tk), 0)
        p_parts, a_parts = [], []
        for j in range(2 * tq // rs):
            rows = slice(j * rs, (j + 1) * rs)
            sj = s[rows]
            if fox:
                ck_row = ck_ref[pl.ds(2 * ik + j * rs // tq, 1), :]
                sj = sj + wide(cq_scr[rows, :]) - ck_row
            if masked:
                sj = jnp.where(col <= row + (j * rs) % tq, sj, NEG_INF)
            m_old = m_scr[rows, :]
            m_new = jnp.maximum(m_old, jnp.max(sj, axis=1, keepdims=True))
            pj = jnp.exp2(sj - wide(m_new))
            alpha = jnp.exp2(m_old - m_new)
            l_scr[rows, :] = alpha * l_scr[rows, :] + jnp.sum(pj, axis=1, keepdims=True)
            m_scr[rows, :] = m_new
            p_parts.append(pj.astype(BF16))
            a_parts.append(alpha)
        p = jnp.concatenate(p_parts, axis=0)
        alpha = jnp.concatenate(a_parts, axis=0)
        pv = _dot_nt(p, vb_scr[ik]) if fox else _dot(p, v_ref[pl.ds(start, tk), :])
        acc_scr[...] = alpha * acc_scr[...] + pv

    def full_block(ik, carry):
        block(ik, False)
        return carry

    lax.fori_loop(0, iq, full_block, 0)
    block(iq, True)
    on = acc_scr[...] / l_scr[...]
    top, bot = on[:tq], on[tq:]
    if fox:
        o_ref[...] = jnp.where(lane < C_HD, top, bot)
    else:
        o = top - _diff_lambda(dl_ref, lam_init) * bot
        o_ref[...] = _rms(o, g_ref[...]) * (1.0 - lam_init)


def _attn_prompt(fox, q, k, v, extra, layer=0, tq=512):
    b, t, w = q.shape
    ng, nq = w // LANES, t // tq
    tile = lambda bi, g, iq: (bi, iq, g)
    stat = pltpu.VMEM((2 * tq, LANES), F32)
    scratch = [pltpu.VMEM((2 * tq, LANES), BF16), stat, stat, stat]
    if fox:
        cq, ck = extra
        kv_spec = pl.BlockSpec((None, None, LANES, t), lambda bi, g, iq: (layer, bi, g, 0))
        extra_specs = [pl.BlockSpec((None, tq, cq.shape[2]), lambda bi, g, iq: (bi, iq, 0)),
                       pl.BlockSpec((None, None, 2 * nq, tq), lambda bi, g, iq: (bi, g, 0, 0))]
        scratch += [stat, pltpu.VMEM((nq, LANES, tq), BF16), pltpu.VMEM((nq, LANES, tq), BF16)]
    else:
        kv_spec = pl.BlockSpec((None, t, LANES), lambda bi, g, iq: (bi, 0, g))
        extra_specs = [pl.BlockSpec((4, A_HD), lambda bi, g, iq: (0, 0)),
                       pl.BlockSpec((1, A_VD), lambda bi, g, iq: (0, 0))]
    return dict(grid=(b, ng, nq), tq=tq,
                in_specs=[pl.BlockSpec((None, tq, LANES), tile), kv_spec, kv_spec] + extra_specs,
                out_spec=pl.BlockSpec((None, tq, LANES), tile),
                out_shape=jax.ShapeDtypeStruct((b, t, w), F32),
                scratch=scratch, args=[q, k, v, *extra])


def _fox_cum_prompt_kernel(lf_ref, lft_ref, cum_ref, cumt_ref):
    cum_ref[...] = _cumsum(lf_ref[...], 0) * LOG2E
    cumt_ref[...] = _cumsum(lft_ref[...], 1) * LOG2E


def _fox_cum_prompt(lf, lft):
    b, t, h = lf.shape
    return pl.pallas_call(
        _fox_cum_prompt_kernel,
        grid=(b,),
        in_specs=[pl.BlockSpec((None, t, h), lambda i: (i, 0, 0)),
                  pl.BlockSpec((h, t), lambda i: (0, i))],
        out_specs=[pl.BlockSpec((None, t, h), lambda i: (i, 0, 0)),
                   pl.BlockSpec((None, h, t), lambda i: (i, 0, 0))],
        out_shape=[jax.ShapeDtypeStruct((b, t, h), F32),
                   jax.ShapeDtypeStruct((b, h, t), F32)],
        compiler_params=_params(("parallel",)),
        name="fox_cum_prompt",
    )(lf, lft)


KEYS_PAD = PAST_LEN + PAGE_SIZE
DIFF_PAGE_ROWS = PAGE_SIZE * A_HEADS
DIFF_NEW_ROWS = DEC_SEQ * A_HEADS


def _softmax_rows(s):
    m = jnp.max(s, axis=1, keepdims=True)
    p = jnp.exp2(s - m)
    return p.astype(BF16), jnp.sum(p, axis=1, keepdims=True)


def _decode_diff_kernel(lam_init, pt_ref, q_ref, kn_ref, vn_ref, dl_ref, g_ref, *refs):
    k_refs, v_refs, o_ref = refs[:N_PAGES], refs[N_PAGES:2 * N_PAGES], refs[2 * N_PAGES]
    q = q_ref[...]
    lane = lax.broadcasted_iota(jnp.int32, q.shape, 1)
    q2 = jnp.concatenate([jnp.where(lane < A_HD, q, 0.0),
                          jnp.where(lane >= A_HD, q, 0.0)], axis=0).astype(BF16)
    pad = jnp.zeros((DIFF_PAGE_ROWS - DIFF_NEW_ROWS, LANES), F32)
    kn = jnp.concatenate([kn_ref[...], pad], axis=0).astype(BF16)
    vn = jnp.concatenate([vn_ref[...], pad], axis=0).astype(BF16)
    parts = [_dot_nt(q2, kr[...].astype(BF16)) for kr in k_refs]
    parts.append(_dot_nt(q2, kn))
    s = jnp.concatenate(parts, axis=1)
    r = lax.broadcasted_iota(jnp.int32, s.shape, 0)
    col = lax.broadcasted_iota(jnp.int32, s.shape, 1)
    t_row = (r % DIFF_NEW_ROWS) // A_HEADS
    valid = (col % A_HEADS == r % A_HEADS) & (col // A_HEADS <= PAST_LEN + t_row)
    pb, l = _softmax_rows(jnp.where(valid, s, NEG_INF))
    o = _dot(pb[:, N_PAGES * DIFF_PAGE_ROWS:], vn)
    for j, vr in enumerate(v_refs):
        o = o + _dot(pb[:, j * DIFF_PAGE_ROWS:(j + 1) * DIFF_PAGE_ROWS], vr[...].astype(BF16))
    on = o / l
    out = on[:DIFF_NEW_ROWS] - _diff_lambda(dl_ref, lam_init) * on[DIFF_NEW_ROWS:]
    o_ref[...] = _rms(out, g_ref[...]) * (1.0 - lam_init)


def _decode_fox_kernel(pt_ref, q_ref, kn_ref, vn_ref, lfn_ref, *refs):
    kt_refs, vt_refs = refs[:N_PAGES], refs[N_PAGES:2 * N_PAGES]
    lf_refs, o_ref = refs[2 * N_PAGES:3 * N_PAGES], refs[3 * N_PAGES]
    nblk = C_HEADS
    rows = DEC_SEQ * nblk
    q4 = q_ref[...]
    qb = jnp.concatenate([jnp.broadcast_to(q4[t:t + 1], (nblk, C_W)) for t in range(DEC_SEQ)], axis=0)
    rblk = lax.broadcasted_iota(jnp.int32, (rows, C_W), 0) % nblk
    lblk = lax.broadcasted_iota(jnp.int32, (rows, C_W), 1) // C_HD
    own = rblk == lblk
    q2 = jnp.where(own, qb, 0.0).astype(BF16)
    pad = jnp.zeros((PAGE_SIZE - DEC_SEQ, C_W), F32)
    kn = jnp.concatenate([kn_ref[...], pad], axis=0).astype(BF16)
    vn = jnp.concatenate([vn_ref[...], pad], axis=0).astype(BF16)
    parts = [_dot(q2, kr[...].astype(BF16)) for kr in kt_refs]
    parts.append(_dot_nt(q2, kn))
    s = jnp.concatenate(parts, axis=1)
    ck = _cumsum(jnp.concatenate([r[...] for r in lf_refs] + [lfn_ref[...]], axis=1), 1)
    ck_rows = jnp.concatenate([ck] * DEC_SEQ, axis=0)
    cq = jnp.concatenate([ck[:, PAST_LEN + t:PAST_LEN + t + 1] for t in range(DEC_SEQ)], axis=0)
    s = s + (cq - ck_rows) * LOG2E
    t_row = lax.broadcasted_iota(jnp.int32, s.shape, 0) // nblk
    col = lax.broadcasted_iota(jnp.int32, s.shape, 1)
    pb, l = _softmax_rows(jnp.where(col <= PAST_LEN + t_row, s, NEG_INF))
    o = _dot(pb[:, PAST_LEN:], vn)
    for j, vr in enumerate(vt_refs):
        o = o + _dot_nt(pb[:, j * PAGE_SIZE:(j + 1) * PAGE_SIZE], vr[...].astype(BF16))
    on = jnp.where(own, o / l, 0.0)
    o_ref[...] = jnp.concatenate([jnp.sum(on[t * nblk:(t + 1) * nblk], axis=0, keepdims=True)
                                  for t in range(DEC_SEQ)], axis=0)


def _decode_attn(layer, per_batch, shared, paged, out_rows):
    bd = per_batch[0].shape[0]
    batch_spec = lambda a: pl.BlockSpec((None,) + a.shape[1:], lambda b, pt: (b, 0, 0))
    shared_spec = lambda a: pl.BlockSpec(a.shape, lambda b, pt: (0, 0))

    def page_spec(a, j):
        return pl.BlockSpec((None, None) + a.shape[2:], lambda b, pt: (layer, pt[b, j], 0, 0))

    out_cols = per_batch[0].shape[2]
    late = [False] * (len(per_batch) + len(shared)) + [n == 1 for n in range(len(paged)) for _ in range(N_PAGES)]
    return dict(late=late,
                in_specs=([batch_spec(a) for a in per_batch] + [shared_spec(a) for a in shared]
                          + [page_spec(a, j) for a in paged for j in range(N_PAGES)]),
                out_spec=pl.BlockSpec((None, out_rows, out_cols), lambda b, pt: (b, 0, 0)),
                out_shape=jax.ShapeDtypeStruct((bd, out_rows, out_cols), F32),
                args=[*per_batch, *shared, *(a for a in paged for _ in range(N_PAGES))])


def _attn_layer_kernel(prompt_kernel, decode_kernel, n_prompt_in, n_decode_in, steps_per_decode,
                       pt_ref, *refs):
    prompt_in = refs[:n_prompt_in]
    decode_in = refs[n_prompt_in:n_prompt_in + n_decode_in]
    o_prompt, o_decode = refs[n_prompt_in + n_decode_in:n_prompt_in + n_decode_in + 2]
    scratch = refs[n_prompt_in + n_decode_in + 2:]
    prompt_kernel(*prompt_in, o_prompt, *scratch)
    step = (pl.program_id(0) * pl.num_programs(1) + pl.program_id(1)) * pl.num_programs(2) + pl.program_id(2)

    @pl.when(step % steps_per_decode == 0)
    def _():
        decode_kernel(pt_ref, *decode_in, o_decode)


def _attn_layer(name, prompt_kernel, prompt, decode_kernel, decode, page_table):
    grid = prompt["grid"]
    steps = grid[0] * grid[1] * grid[2]
    bd = decode["out_shape"].shape[0]
    per = steps // bd

    def lift_prompt(spec):
        f = spec.index_map
        return pl.BlockSpec(spec.block_shape, lambda bi, g, iq, pt: f(bi, g, iq))

    def lift_decode(spec, late=False):
        f, shift = spec.index_map, (per // 2 if late else 0)
        return pl.BlockSpec(
            spec.block_shape,
            lambda bi, g, iq, pt: f(jnp.minimum((((bi * grid[1] + g) * grid[2] + iq) + shift) // per, bd - 1), pt))

    return pl.pallas_call(
        functools.partial(_attn_layer_kernel, prompt_kernel, decode_kernel,
                          len(prompt["in_specs"]), len(decode["in_specs"]), per),
        grid_spec=pltpu.PrefetchScalarGridSpec(
            num_scalar_prefetch=1,
            grid=grid,
            in_specs=[lift_prompt(s) for s in prompt["in_specs"]]
            + [lift_decode(s, late) for s, late in zip(decode["in_specs"], decode["late"])],
            out_specs=[lift_prompt(prompt["out_spec"]), lift_decode(decode["out_spec"])],
            scratch_shapes=prompt["scratch"]),
        out_shape=[prompt["out_shape"], decode["out_shape"]],
        compiler_params=_params(("arbitrary", "arbitrary", "arbitrary")),
        name=name,
    )(page_table, *prompt["args"], *decode["args"])


def _pool_project(d_list, pw_ref, ps_ref):
    ps = ps_ref[...]
    outs = [_dot(d.astype(BF16), pw_ref[g]) * ps[:, g * POOL_GC:(g + 1) * POOL_GC]
            for g, d in enumerate(d_list)]
    return jnp.concatenate(outs, axis=1)


def _pool_prompt_kernel(p_ref, pw_ref, ps_ref, o_ref):
    t = p_ref.shape[0]
    row = lax.broadcasted_iota(jnp.int32, (t, POOL_GC), 0)
    d_list = []
    for g, w in enumerate(POOL_WINDOWS):
        x = p_ref[:, g * POOL_GC:(g + 1) * POOL_GC]
        acc, s = x, 1
        while s < w:
            acc = acc + _shift_down(acc, s, row)
            s *= 2
        cnt = jnp.minimum(w, row[:, 0:1] + 1).astype(F32)
        d_list.append(acc / cnt - x)
    o_ref[...] = _pool_project(d_list, pw_ref, ps_ref)


def _pool_prompt(p, pool_w, pool_scale):
    b, t, c = p.shape
    return pl.pallas_call(
        _pool_prompt_kernel,
        grid=(b,),
        in_specs=[pl.BlockSpec((None, t, c), lambda i: (i, 0, 0)),
                  pl.BlockSpec(pool_w.shape, lambda i: (0, 0, 0)),
                  pl.BlockSpec((1, c), lambda i: (0, 0))],
        out_specs=pl.BlockSpec((None, t, c), lambda i: (i, 0, 0)),
        out_shape=jax.ShapeDtypeStruct((b, t, c), F32),
        compiler_params=_params(("parallel",)),
        name="pool_prompt",
    )(p, pool_w, pool_scale)


def _pool_sample_kernel(p_ref, pre_ref, pw_ref, ps_ref, o_ref):
    for t in range(DEC_SEQ):
        d_list = []
        for g, w in enumerate(POOL_WINDOWS):
            sl = slice(g * POOL_GC, (g + 1) * POOL_GC)
            x = p_ref[t, :, sl]
            acc = x
            for j in range(1, w):
                src = t - j
                acc = acc + (p_ref[src, :, sl] if src >= 0 else pre_ref[POOL_BUF + src, :, sl])
            cnt = float(min(w, PAST_LEN + t + 1))
            d_list.append(acc / cnt - x)
        o_ref[t] = _pool_project(d_list, pw_ref, ps_ref)


def _pool_sample(p_tm, prefix_tm, pool_w, pool_scale):
    return pl.pallas_call(
        _pool_sample_kernel,
        out_shape=jax.ShapeDtypeStruct(p_tm.shape, F32),
        compiler_params=pltpu.CompilerParams(vmem_limit_bytes=VMEM_LIMIT),
        name="pool_sample",
    )(p_tm, prefix_tm, pool_w, pool_scale)


def _out_proj_kernel(nparts, x_ref, g_ref, w_ref, *refs):
    parts, o_ref = refs[:nparts], refs[nparts]
    y, off = None, 0
    for a_ref in parts:
        n = a_ref.shape[1]
        d = _dot(a_ref[...].astype(BF16), w_ref[off:off + n, :])
        y = d if y is None else y + d
        off += n
    o_ref[...] = x_ref[...] + _rms(y, g_ref[...])


def _out_proj(x, g, w, parts, tm):
    m = x.shape[0]
    tok = lambda i: (i, 0)
    return pl.pallas_call(
        functools.partial(_out_proj_kernel, len(parts)),
        grid=(m // tm,),
        in_specs=[pl.BlockSpec((tm, D_MODEL), tok),
                  pl.BlockSpec((1, D_MODEL), lambda i: (0, 0)),
                  pl.BlockSpec(w.shape, lambda i: (0, 0))]
        + [pl.BlockSpec((tm, a.shape[1]), tok) for a in parts],
        out_specs=pl.BlockSpec((tm, D_MODEL), tok),
        out_shape=jax.ShapeDtypeStruct((m, D_MODEL), F32),
        compiler_params=_params(("parallel",)),
        name="out_proj",
    )(x, g, w, *parts)


MXU_TILE = 256
FFN_CHUNKS = ((0, 6 * MXU_TILE), (6 * MXU_TILE, 5 * MXU_TILE))
RESIDENT = pl.Buffered(1)


def _ffn_chunks(h, prev_rows, wu_ref, cw_ref, cb_ref, wd_ref, tail_ref, carry_ref=None):
    m = h.shape[0]
    y = None
    for c0, width in FFN_CHUNKS:
        def conv(off):
            cols = slice(off + c0, off + c0 + width)
            u = _dot(h, wu_ref[:, cols])
            um2, um1 = prev_rows(u, cols)
            if carry_ref is not None:
                carry_ref[:, cols] = u[m - 8:]
            tail_ref[:, cols] = u[m - tail_ref.shape[0]:]
            cw = cw_ref[:, cols]
            return cb_ref[:, cols] + cw[0:1] * um2 + cw[1:2] * um1 + cw[2:3] * u

        cg = conv(0)
        cv = conv(D_FF)
        act = (cg * jax.nn.sigmoid(cg) * cv).astype(BF16)
        d = _dot(act, wd_ref[c0:c0 + width, :])
        y = d if y is None else y + d
    return y


def _ffn_prompt_kernel(nparts, x_ref, gmix_ref, wout_ref, *refs):
    parts = refs[:nparts]
    gpre_ref, gpost_ref, wu_ref, cw_ref, cb_ref, wd_ref, o_ref, tail_ref, carry = refs[nparts:]
    it = pl.program_id(1)

    @pl.when(it == 0)
    def _():
        carry[...] = jnp.zeros(carry.shape, F32)

    def prev_rows(u, cols):
        row = lax.broadcasted_iota(jnp.int32, u.shape, 0)
        prev = carry[:, cols]
        p1, p2 = prev[7:8], prev[6:7]
        um1 = jnp.where(row >= 1, pltpu.roll(u, 1, 0), p1)
        um2 = jnp.where(row >= 2, pltpu.roll(u, 2, 0), jnp.where(row == 0, p2, p1))
        return um2, um1

    mix, off = None, 0
    for a_ref in parts:
        n = a_ref.shape[1]
        d = _dot(a_ref[...].astype(BF16), wout_ref[off:off + n, :])
        mix = d if mix is None else mix + d
        off += n
    x = x_ref[...] + _rms(mix, gmix_ref[...])
    h = _rms(x, gpre_ref[...]).astype(BF16)
    y = _ffn_chunks(h, prev_rows, wu_ref, cw_ref, cb_ref, wd_ref, tail_ref, carry)
    o_ref[...] = x + _rms(y, gpost_ref[...])


def _ffn_prompt(x, parts, gmix, w_out, gpre, gpost, w_up, conv_w, conv_b, w_down, tm=512):
    b, t, _ = x.shape
    tok = lambda bi, it: (bi, it, 0)
    const = lambda bi, it: (0, 0)
    whole = lambda a: pl.BlockSpec(a.shape, const, pipeline_mode=RESIDENT)
    return pl.pallas_call(
        functools.partial(_ffn_prompt_kernel, len(parts)),
        grid=(b, t // tm),
        in_specs=[pl.BlockSpec((None, tm, D_MODEL), tok), pl.BlockSpec((1, D_MODEL), const), whole(w_out)]
        + [pl.BlockSpec((None, tm, a.shape[2]), tok) for a in parts]
        + [pl.BlockSpec((1, D_MODEL), const), pl.BlockSpec((1, D_MODEL), const),
           whole(w_up), whole(conv_w), whole(conv_b), whole(w_down)],
        out_specs=[pl.BlockSpec((None, tm, D_MODEL), tok),
                   pl.BlockSpec((None, None, CONV_W - 1, 2 * D_FF), lambda bi, it: (bi, it, 0, 0))],
        out_shape=[jax.ShapeDtypeStruct((b, t, D_MODEL), F32),
                   jax.ShapeDtypeStruct((b, t // tm, CONV_W - 1, 2 * D_FF), F32)],
        scratch_shapes=[pltpu.VMEM((8, 2 * D_FF), F32)],
        compiler_params=_params(("arbitrary", "arbitrary")),
        name="ffn_prompt",
    )(x, gmix, w_out, *parts, gpre, gpost, w_up, conv_w, conv_b, w_down)


def _ffn_sample_kernel(x_ref, gpre_ref, gpost_ref, wu_ref, cw_ref, cb_ref, wd_ref, pre_ref, o_ref, tail_ref):
    m = x_ref.shape[0]
    bd = m // DEC_SEQ

    def prev_rows(u, cols):
        full = jnp.concatenate([pre_ref[:, cols], u], axis=0)
        return full[:m], full[bd:bd + m]

    h = _rms(x_ref[...], gpre_ref[...]).astype(BF16)
    y = _ffn_chunks(h, prev_rows, wu_ref, cw_ref, cb_ref, wd_ref, tail_ref)
    o_ref[...] = x_ref[...] + _rms(y, gpost_ref[...])


def _ffn_sample(x, gpre, gpost, w_up, conv_w, conv_b, w_down, prefix_tm):
    return pl.pallas_call(
        _ffn_sample_kernel,
        out_shape=[jax.ShapeDtypeStruct(x.shape, F32), jax.ShapeDtypeStruct(prefix_tm.shape, F32)],
        compiler_params=pltpu.CompilerParams(vmem_limit_bytes=VMEM_LIMIT),
        name="ffn_sample",
    )(x, gpre, gpost, w_up, conv_w, conv_b, w_down, prefix_tm)


def _rope_tables(pos):
    half = ROT_DIM // 2
    inv = ROPE_THETA ** (-jnp.arange(0, ROT_DIM, 2, dtype=F32) / ROT_DIM)
    ang = pos.astype(F32)[:, None] * inv[None, :]
    cos, sin = jnp.cos(ang), jnp.sin(ang)
    n = pos.shape[0]
    pad = jnp.zeros((n, A_HD - ROT_DIM), F32)
    zero = jnp.zeros((n, half), F32)
    c = jnp.concatenate([cos, cos, pad + 1.0], axis=1)
    s1 = jnp.concatenate([zero, sin, pad], axis=1)
    s2 = jnp.concatenate([-sin, zero, pad], axis=1)
    return tuple(jnp.concatenate([a, a], axis=1) for a in (c, s1, s2))


def _to_bm(a_tm):
    return a_tm.reshape(DEC_SEQ, DEC_BATCH, -1).transpose(1, 0, 2)


def _to_tm(a_bm):
    return a_bm.transpose(1, 0, 2).reshape(DEC_SEQ * DEC_BATCH, -1)


def kernel(x_prompt, x_sample, cache_diff_k, cache_diff_v, state_pool, cache_fox_k, cache_fox_v, cache_fox_logf, state_ffn_conv, page_table, w_in_ab, diff_lambda, diff_subln_g, pool_w, pool_scale, w_out_ab, w_in_c, b_f, w_out_c, norm_mix_pre, norm_mix_post, norm_ffn_pre, norm_ffn_post, w_up, conv_w, conv_b, w_down):
    n_phys = cache_diff_k.shape[1]
    ck_diff = cache_diff_k.reshape(-1, n_phys, DIFF_PAGE_ROWS, LANES)
    cv_diff = cache_diff_v.reshape(-1, n_phys, DIFF_PAGE_ROWS, LANES)
    ck_fox = cache_fox_k.transpose(0, 1, 3, 4, 2).reshape(-1, n_phys, C_W, PAGE_SIZE)
    cv_fox = cache_fox_v.transpose(0, 1, 3, 4, 2).reshape(-1, n_phys, C_W, PAGE_SIZE)
    lf_fox = cache_fox_logf.transpose(0, 1, 3, 2)

    tabs_p = _rope_tables(jnp.arange(SEQ))
    tabs_s = _rope_tables(PAST_LEN + jnp.repeat(jnp.arange(DEC_SEQ), DEC_BATCH))

    xp = x_prompt.reshape(BATCH * SEQ, D_MODEL)
    xs = x_sample.transpose(1, 0, 2).reshape(DEC_SEQ * DEC_BATCH, D_MODEL)
    tm_p, tm_s = 512, DEC_SEQ * DEC_BATCH
    row = lambda a: a.reshape(1, -1)
    out = {n: [] for n in ("pl_p", "fl_p", "cv_p", "pl_s", "fl_s", "cv_s")}
    diff_p, diff_s, fox_p, fox_s = [], [], [], []

    for l in range(DEPTH):
        i = l // 2
        if l % 2 == 0:
            lam_init = 0.8 - 0.6 * math.exp(-0.3 * l)
            w_in = w_in_ab[i].astype(BF16)
            w_out = w_out_ab[i].astype(BF16)
            pw = pool_w[i].astype(BF16)
            small = [diff_lambda[i], row(diff_subln_g[i])]
            q, k, v, p, *diff_p = _inproj_even(xp, row(norm_mix_pre[l]), w_in, tabs_p, tm_p, i, diff_p)
            qs, _, _, ps, *diff_s = _inproj_even(xs, row(norm_mix_pre[l]), w_in, tabs_s, tm_s, i, diff_s)
            b3 = lambda a: a.reshape(BATCH, SEQ, -1)
            th = lambda a: _to_bm(a).reshape(DEC_BATCH, DIFF_NEW_ROWS, LANES)
            tq = 512
            o, os_ = _attn_layer(
                "diff_attn",
                functools.partial(_attn_prompt_kernel, False, lam_init, tq, tq),
                _attn_prompt(False, b3(q), b3(k), b3(v), small, tq=tq),
                functools.partial(_decode_diff_kernel, lam_init),
                _decode_attn(i, [th(qs.astype(F32)), th(diff_s[0][i]), th(diff_s[1][i])], small,
                             [ck_diff, cv_diff], DIFF_NEW_ROWS),
                page_table)
            pooled = _pool_prompt(b3(p), pw, row(pool_scale[i]))
            mix_p = [o, pooled]
            out["pl_p"].append(b3(p)[:, SEQ - POOL_BUF:])
            os_ = os_.reshape(DEC_BATCH, DEC_SEQ, A_V_W)
            pooled = _pool_sample(ps.reshape(DEC_SEQ, DEC_BATCH, POOL_W),
                                  state_pool[i].transpose(1, 0, 2), pw, row(pool_scale[i]))
            xs = _out_proj(xs, row(norm_mix_post[l]), w_out,
                           [_to_tm(os_), pooled.reshape(-1, POOL_W)], tm_s)
            out["pl_s"].append(jnp.concatenate([state_pool[i], _to_bm(ps)], axis=1)[:, -POOL_BUF:])
        else:
            w_c = w_in_c[i]
            w_q = w_c[:, :C_W].astype(BF16)
            w_kt = w_c[:, C_W:2 * C_W].T.astype(BF16)
            w_vt = w_c[:, 2 * C_W:3 * C_W].T.astype(BF16)
            w_f = w_c[:, 3 * C_W:].astype(BF16)
            w_out = w_out_c[i].astype(BF16)
            odd_args = (row(norm_mix_pre[l]), w_q, w_kt, w_vt, w_f, w_f.T, b_f[i])
            q, lf, lft, *fox_p = _inproj_odd(xp, *odd_args, tm_p, SEQ, i, fox_p)
            qs, lfs, lfts, *fox_s = _inproj_odd(xs, *odd_args, tm_s, DEC_BATCH, i, fox_s)
            b3 = lambda a: a.reshape(BATCH, SEQ, -1)
            cum, cumt = _fox_cum_prompt(b3(lf), lft)
            tq = 512
            ck =(cumt.reshape(BATCH, C_HEADS // 2, 2, SEQ // tq, tq).transpose(0, 1, 3, 2, 4)
                  .reshape(BATCH, C_HEADS // 2, 2 * SEQ // tq, tq))
            kb, vb = (a[i].transpose(2, 0, 1) for a in fox_s)
            lfn = lfts.reshape(C_HEADS, DEC_SEQ, DEC_BATCH).transpose(2, 0, 1)
            lfn = jnp.pad(lfn, ((0, 0), (0, 0), (0, PAGE_SIZE - DEC_SEQ)))
            o, os_ = _attn_layer(
                "fox_attn",
                functools.partial(_attn_prompt_kernel, True, 0.0, tq, tq),
                _attn_prompt(True, b3(q), fox_p[0], fox_p[1], [cum, ck], i, tq),
                _decode_fox_kernel,
                _decode_attn(i, [_to_bm(qs).astype(F32), kb, vb, lfn], [], [ck_fox, cv_fox, lf_fox], DEC_SEQ),
                page_table)
            mix_p = [o]
            out["fl_p"].append(b3(lf))
            xs = _out_proj(xs, row(norm_mix_post[l]), w_out, [_to_tm(os_)], tm_s)
            out["fl_s"].append(_to_bm(lfs))

        wu, wd = w_up[l].astype(BF16), w_down[l].astype(BF16)
        ffn_args = (row(norm_ffn_pre[l]), row(norm_ffn_post[l]), wu, conv_w[l], row(conv_b[l]), wd)
        xp3, tail = _ffn_prompt(xp.reshape(BATCH, SEQ, D_MODEL), mix_p, row(norm_mix_post[l]), w_out, *ffn_args)
        xp = xp3.reshape(BATCH * SEQ, D_MODEL)
        out["cv_p"].append(tail[:, -1])
        prefix_tm = state_ffn_conv[l].transpose(1, 0, 2).reshape((CONV_W - 1) * DEC_BATCH, 2 * D_FF)
        xs, tail = _ffn_sample(xs, *ffn_args, prefix_tm)
        out["cv_s"].append(tail.reshape(CONV_W - 1, DEC_BATCH, 2 * D_FF).transpose(1, 0, 2))

    st = {n: jnp.stack(v) for n, v in out.items()}
    n_even, n_odd = len(diff_p[0]), len(fox_p[0])
    dk_p, dv_p = (a.reshape(n_even, BATCH, SEQ, A_HEADS, A_VD) for a in diff_p)
    fk_p, fv_p = (a.reshape(n_odd, BATCH, C_HEADS, C_HD, SEQ).transpose(0, 1, 4, 2, 3) for a in fox_p)
    dk_s, dv_s = (a.reshape(n_even, DEC_SEQ, DEC_BATCH, A_HEADS, A_VD).transpose(0, 2, 1, 3, 4) for a in diff_s)
    fk_s, fv_s = (a.reshape(n_odd, DEC_SEQ, C_HEADS, C_HD, DEC_BATCH).transpose(0, 4, 1, 2, 3) for a in fox_s)
    y_p = xp.reshape(BATCH, SEQ, D_MODEL)
    y_s = xs.reshape(DEC_SEQ, DEC_BATCH, D_MODEL).transpose(1, 0, 2)
    return (y_p, y_s, dk_p, dv_p, st["pl_p"], fk_p, fv_p, st["fl_p"], st["cv_p"],
            dk_s, dv_s, st["pl_s"], fk_s, fv_s, st["fl_s"], st["cv_s"])
```
